```python
import math
import jax, jax.numpy as jnp
from jax import lax
import numpy as np

D_MODEL = 1024
BATCH = 4
SEQ = 4096
DEPTH = 4
DEC_BATCH = 128
DEC_SEQ = 4
PAST_LEN = 8192
PAGE_SIZE = 128

HEAD_DIM = 64
N_MIXERS = 3
SWA_HEADS = D_MODEL // HEAD_DIM
SWA_KV_HEADS = SWA_HEADS // 4
SWA_GROUP = SWA_HEADS // SWA_KV_HEADS
WINDOW = 128
SSM_D_INNER = 2 * D_MODEL
SSM_HEADDIM = 64
SSM_HEADS = SSM_D_INNER // SSM_HEADDIM
SSM_GROUPS = 8
SSM_D_STATE = 128
SSM_CONV = 4
SSM_CHUNK = 128
SSM_CONV_DIM = SSM_D_INNER + 2 * SSM_GROUPS * SSM_D_STATE
SB_HEADS = D_MODEL // HEAD_DIM
SB_BLOCK = 128
SB_BIAS_INIT = 8.0
N_MEM = 256
MEM_HEADS = 4
D_FF = 2816
FFN_CONV = 3
EPS = 1e-6
N_A = (DEPTH + 2) // 3
N_B = (DEPTH + 1) // 3
N_C = DEPTH // 3

kernel_name = 'hybrid_swa_ssd_stickbreak_decode_step'


def rmsnorm(x, w):
    xf = x.astype(jnp.float32)
    y = xf * lax.rsqrt(jnp.mean(xf * xf, axis=-1, keepdims=True) + EPS)
    return (y * w.astype(jnp.float32)).astype(x.dtype)


def causal_dwconv(x_ext, w, b):
    width = w.shape[0]
    length = x_ext.shape[1] - (width - 1)
    return b + sum(x_ext[:, t:t + length] * w[t] for t in range(width))


def alibi_slopes(n_heads):
    return jnp.exp2(-8.0 * jnp.arange(1, n_heads + 1, dtype=jnp.float32) / n_heads)


def swa_qkv(h, wqkv):
    b, L, _ = h.shape
    nq = SWA_HEADS * HEAD_DIM
    nk = SWA_KV_HEADS * HEAD_DIM
    qkv = h @ wqkv
    q = qkv[..., :nq].reshape(b, L, SWA_KV_HEADS, SWA_GROUP, HEAD_DIM)
    k = qkv[..., nq:nq + nk].reshape(b, L, SWA_KV_HEADS, HEAD_DIM)
    v = qkv[..., nq + nk:].reshape(b, L, SWA_KV_HEADS, HEAD_DIM)
    return q, k, v


def swa_attend(q, k, v, qpos, kpos, sinks):
    s = jnp.einsum('bnqgrd,bnkgd->bngrqk', q, k).astype(jnp.float32) / math.sqrt(HEAD_DIM)
    dist = (qpos[:, :, None] - kpos[:, None, :]).astype(jnp.float32)
    valid = (dist >= 0) & (dist <= WINDOW) & (kpos[:, None, :] >= 0)
    slopes = alibi_slopes(SWA_HEADS).reshape(SWA_KV_HEADS, SWA_GROUP)[None, None, :, :, None, None]
    s = jnp.where(valid[None, :, None, None], s - slopes * dist[None, :, None, None], -jnp.inf)
    sink = sinks.astype(jnp.float32).reshape(SWA_KV_HEADS, SWA_GROUP)[None, None, :, :, None, None]
    sink = jnp.broadcast_to(sink, s.shape[:-1] + (1,))
    p = jax.nn.softmax(jnp.concatenate([s, sink], axis=-1), axis=-1)[..., :-1]
    return jnp.einsum('bngrqk,bnkgd->bnqgrd', p.astype(v.dtype), v)


def swa_prompt(h, wqkv, sinks, wo):
    b, L, _ = h.shape
    nb = L // WINDOW
    q, k, v = swa_qkv(h, wqkv)
    qb = q.reshape(b, nb, WINDOW, SWA_KV_HEADS, SWA_GROUP, HEAD_DIM)
    kb = k.reshape(b, nb, WINDOW, SWA_KV_HEADS, HEAD_DIM)
    vb = v.reshape(b, nb, WINDOW, SWA_KV_HEADS, HEAD_DIM)
    shift = ((0, 0), (1, 0), (0, 0), (0, 0), (0, 0))
    kcat = jnp.concatenate([jnp.pad(kb, shift)[:, :-1], kb], axis=2)
    vcat = jnp.concatenate([jnp.pad(vb, shift)[:, :-1], vb], axis=2)
    blk = jnp.arange(nb)[:, None] * WINDOW
    qpos = blk + jnp.arange(WINDOW)[None]
    kpos = blk - WINDOW + jnp.arange(2 * WINDOW)[None]
    o = swa_attend(qb, kcat, vcat, qpos, kpos, sinks)
    return o.reshape(b, L, SWA_HEADS * HEAD_DIM) @ wo, k[:, -WINDOW:], v[:, -WINDOW:]


def swa_sample(h, k_buf, v_buf, wqkv, sinks, wo):
    b, L, _ = h.shape
    q, k, v = swa_qkv(h, wqkv)
    kcat = jnp.concatenate([k_buf, k], axis=1)
    vcat = jnp.concatenate([v_buf, v], axis=1)
    qpos = (WINDOW + jnp.arange(L))[None]
    kpos = jnp.arange(WINDOW + L)[None]
    o = swa_attend(q[:, None], kcat[:, None], vcat[:, None], qpos, kpos, sinks)
    return o.reshape(b, L, SWA_HEADS * HEAD_DIM) @ wo, kcat[:, -WINDOW:], vcat[:, -WINDOW:]


def ssd_scan(x, dt, A, Bm, Cm, init_state, chunk):
    b, L, H, P = x.shape
    G, N = Bm.shape[2], Bm.shape[3]
    R = H // G
    c = L // chunk
    f32 = jnp.float32
    xdt = (x.astype(f32) * dt[..., None]).reshape(b, c, chunk, G, R, P)
    a_cs = jnp.cumsum((dt * A).reshape(b, c, chunk, G, R), axis=2)
    Bc = Bm.astype(f32).reshape(b, c, chunk, G, N)
    Cc = Cm.astype(f32).reshape(b, c, chunk, G, N)
    causal = jnp.tril(jnp.ones((chunk, chunk), dtype=bool))[None, None, :, :, None, None]
    seg = a_cs[:, :, :, None] - a_cs[:, :, None, :]
    decay = jnp.exp(jnp.where(causal, seg, -jnp.inf))
    cb = jnp.einsum('bclgn,bcsgn->bclsg', Cc, Bc)
    y_diag = jnp.einsum('bclsg,bclsgr,bcsgrp->bclgrp', cb, decay, xdt)
    to_end = jnp.exp(a_cs[:, :, -1:] - a_cs)
    chunk_states = jnp.einsum('bclgn,bclgr,bclgrp->bcgrpn', Bc, to_end, xdt)
    chunk_decay = jnp.exp(a_cs[:, :, -1])

    def step(state, inp):
        cs, cd = inp
        return state * cd[..., None, None] + cs, state

    final, prev = lax.scan(step, init_state.astype(f32).reshape(b, G, R, P, N),
                           (jnp.moveaxis(chunk_states, 1, 0), jnp.moveaxis(chunk_decay, 1, 0)))
    prev = jnp.moveaxis(prev, 0, 1)
    y_off = jnp.einsum('bclgn,bcgrpn,bclgr->bclgrp', Cc, prev, jnp.exp(a_cs))
    y = (y_diag + y_off).reshape(b, L, H, P)
    return y, final.reshape(b, H, P, N)


def mamba_mixer(h, conv_hist, ssm_state, w_in, conv_w, conv_b, dt_bias, a_log, d_skip, norm_w, w_out):
    b, L, _ = h.shape
    zxbcdt = h @ w_in
    z = zxbcdt[..., :SSM_D_INNER]
    xbc = zxbcdt[..., SSM_D_INNER:SSM_D_INNER + SSM_CONV_DIM]
    dt = zxbcdt[..., SSM_D_INNER + SSM_CONV_DIM:]
    ext = jnp.concatenate([conv_hist, xbc], axis=1)
    xbc = jax.nn.silu(causal_dwconv(ext, conv_w, conv_b))
    gn = SSM_GROUPS * SSM_D_STATE
    xh = xbc[..., :SSM_D_INNER].reshape(b, L, SSM_HEADS, SSM_HEADDIM)
    Bm = xbc[..., SSM_D_INNER:SSM_D_INNER + gn].reshape(b, L, SSM_GROUPS, SSM_D_STATE)
    Cm = xbc[..., SSM_D_INNER + gn:].reshape(b, L, SSM_GROUPS, SSM_D_STATE)
    dt = jax.nn.softplus(dt.astype(jnp.float32) + dt_bias.astype(jnp.float32))
    A = -jnp.exp(a_log.astype(jnp.float32))
    y, new_state = ssd_scan(xh, dt, A, Bm, Cm, ssm_state, math.gcd(L, SSM_CHUNK))
    y = y + xh.astype(jnp.float32) * d_skip.astype(jnp.float32)[:, None]
    y = y.reshape(b, L, SSM_D_INNER).astype(h.dtype)
    y = rmsnorm(y * jax.nn.silu(z), norm_w)
    return y @ w_out, ext[:, -(SSM_CONV - 1):], new_state.astype(ssm_state.dtype)


def sb_qkv(h, wqkv):
    b, L, _ = h.shape
    qkv = (h @ wqkv).reshape(b, L, 3, SB_HEADS, HEAD_DIM)
    return qkv[:, :, 0], qkv[:, :, 1], qkv[:, :, 2]


def stick_breaking_weights(z, valid):
    log_keep = jnp.where(valid, jax.nn.log_sigmoid(-z), 0.0)
    later = lax.cumsum(log_keep, axis=z.ndim - 1, reverse=True) - log_keep
    return jnp.where(valid, jnp.exp(jax.nn.log_sigmoid(z) + later), 0.0)


def sb_prompt(h, wqkv, logit_bias, wo):
    b, L, _ = h.shape
    q, k, v = sb_qkv(h, wqkv)
    nb = L // SB_BLOCK
    qb = jnp.moveaxis(q.reshape(b, nb, SB_BLOCK, SB_HEADS, HEAD_DIM), 1, 0)
    kpos = jnp.arange(L)
    bias = logit_bias.astype(jnp.float32)[None, :, None, None]

    def block(args):
        qi, n = args
        z = jnp.einsum('bqhd,bkhd->bhqk', qi, k).astype(jnp.float32) / math.sqrt(HEAD_DIM) + bias
        qpos = n * SB_BLOCK + jnp.arange(SB_BLOCK)
        a = stick_breaking_weights(z, kpos[None, :] < qpos[:, None])
        return jnp.einsum('bhqk,bkhd->bqhd', a.astype(v.dtype), v)

    o = lax.map(block, (qb, jnp.arange(nb)))
    o = jnp.moveaxis(o, 0, 1).reshape(b, L, SB_HEADS * HEAD_DIM)
    return o @ wo, k, v


def sb_sample(h, pool_k, pool_v, page_table, wqkv, logit_bias, wo):
    b, L, _ = h.shape
    q, k, v = sb_qkv(h, wqkv)
    past = page_table.shape[1] * PAGE_SIZE
    kpos = jnp.arange(past + L)
    valid = kpos[None, :] < (past + jnp.arange(L))[:, None]
    bias = logit_bias.astype(jnp.float32)[:, None, None]

    def one(args):
        qi, ki, vi, pages = args
        kc = jnp.concatenate([pool_k[pages].reshape(past, SB_HEADS, HEAD_DIM), ki], axis=0)
        vc = jnp.concatenate([pool_v[pages].reshape(past, SB_HEADS, HEAD_DIM), vi], axis=0)
        z = jnp.einsum('qhd,khd->hqk', qi, kc).astype(jnp.float32) / math.sqrt(HEAD_DIM) + bias
        a = stick_breaking_weights(z, valid)
        return jnp.einsum('hqk,khd->qhd', a.astype(vc.dtype), vc)

    o = lax.map(one, (q, k, v, page_table))
    return o.reshape(b, L, SB_HEADS * HEAD_DIM) @ wo, k, v


def memory_kv(mem_n, wkv):
    b, m, _ = mem_n.shape
    kv = (mem_n @ wkv).reshape(b, m, 2, MEM_HEADS, HEAD_DIM)
    return kv[:, :, 0], kv[:, :, 1]


def memory_attend(h, mk, mv, wq, wo):
    b, L, _ = h.shape
    q = (h @ wq).reshape(b, L, MEM_HEADS, HEAD_DIM)
    s = jnp.einsum('blhd,bmhd->bhlm', q, mk).astype(jnp.float32) / math.sqrt(HEAD_DIM)
    p = jax.nn.softmax(s, axis=-1)
    o = jnp.einsum('bhlm,bmhd->blhd', p.astype(mv.dtype), mv)
    return o.reshape(b, L, MEM_HEADS * HEAD_DIM) @ wo


def conv_glu_ffn(h, hist, w_in, conv_w, conv_b, w_out):
    gu = h @ w_in
    g, u = gu[..., :D_FF], gu[..., D_FF:]
    ext = jnp.concatenate([hist, g], axis=1)
    g = causal_dwconv(ext, conv_w, conv_b)
    return (jax.nn.silu(g) * u) @ w_out, ext[:, -(FFN_CONV - 1):]


def setup_inputs(seed: int = 0) -> dict:
    key = jax.random.key(seed)
    keys = iter(jax.random.split(key, 64))
    f32 = jnp.float32
    d = D_MODEL

    def normal(shape, scale=1.0):
        return jax.random.normal(next(keys), shape, f32) * scale

    def gain(shape):
        return 1.0 + 0.01 * jax.random.normal(next(keys), shape, f32)

    n_pages = PAST_LEN // PAGE_SIZE
    n_used = DEC_BATCH * n_pages
    n_pool = n_used + n_used // 4
    page_table = jax.random.permutation(next(keys), n_pool)[:n_used].reshape(DEC_BATCH, n_pages).astype(jnp.int32)
    dt0 = jnp.exp(jax.random.uniform(next(keys), (N_B, SSM_HEADS), f32, math.log(1e-3), math.log(1e-1)))
    ssm_dt_bias = dt0 + jnp.log(-jnp.expm1(-dt0))
    ssm_a_log = jnp.log(jax.random.uniform(next(keys), (N_B, SSM_HEADS), f32, 1.0, 16.0))
    swa_in = (SWA_HEADS + 2 * SWA_KV_HEADS) * HEAD_DIM
    ssm_in = 2 * SSM_D_INNER + 2 * SSM_GROUPS * SSM_D_STATE + SSM_HEADS
    mem_w = MEM_HEADS * HEAD_DIM
    return {
        'x_prompt': normal((BATCH, SEQ, d)),
        'x_sample': normal((DEC_BATCH, DEC_SEQ, d)),
        'mem_prompt': normal((BATCH, N_MEM, d)),
        'cache_swa_k': normal((N_A, DEC_BATCH, WINDOW, SWA_KV_HEADS, HEAD_DIM)),
        'cache_swa_v': normal((N_A, DEC_BATCH, WINDOW, SWA_KV_HEADS, HEAD_DIM)),
        'state_ssm': normal((N_B, DEC_BATCH, SSM_HEADS, SSM_HEADDIM, SSM_D_STATE), 0.1),
        'state_ssm_conv': normal((N_B, DEC_BATCH, SSM_CONV - 1, SSM_CONV_DIM)),
        'cache_sb_k': normal((N_C, n_pool, PAGE_SIZE, SB_HEADS, HEAD_DIM)),
        'cache_sb_v': normal((N_C, n_pool, PAGE_SIZE, SB_HEADS, HEAD_DIM)),
        'cache_mem_k': normal((DEPTH, DEC_BATCH, N_MEM, MEM_HEADS, HEAD_DIM)),
        'cache_mem_v': normal((DEPTH, DEC_BATCH, N_MEM, MEM_HEADS, HEAD_DIM)),
        'state_ffn_conv': normal((DEPTH, DEC_BATCH, FFN_CONV - 1, D_FF)),
        'page_table': page_table,
        'norm_mix_w': gain((DEPTH, d)),
        'norm_xattn_w': gain((DEPTH, d)),
        'norm_ffn_w': gain((DEPTH, d)),
        'final_norm_w': gain((d,)),
        'mem_token_norm_w': gain((d,)),
        'swa_wqkv': normal((N_A, d, swa_in), d ** -0.5),
        'swa_sinks': normal((N_A, SWA_HEADS), 0.5),
        'swa_wo': normal((N_A, SWA_HEADS * HEAD_DIM, d), (SWA_HEADS * HEAD_DIM) ** -0.5),
        'ssm_w_in': normal((N_B, d, ssm_in), d ** -0.5),
        'ssm_conv_w': normal((N_B, SSM_CONV, SSM_CONV_DIM), SSM_CONV ** -0.5),
        'ssm_conv_b': normal((N_B, SSM_CONV_DIM), 0.02),
        'ssm_dt_bias': ssm_dt_bias,
        'ssm_a_log': ssm_a_log,
        'ssm_d': gain((N_B, SSM_HEADS)),
        'ssm_norm_w': gain((N_B, SSM_D_INNER)),
        'ssm_w_out': normal((N_B, SSM_D_INNER, d), SSM_D_INNER ** -0.5),
        'sb_wqkv': normal((N_C, d, 3 * SB_HEADS * HEAD_DIM), d ** -0.5),
        'sb_logit_bias': normal((N_C, SB_HEADS), 0.5) - SB_BIAS_INIT,
        'sb_wo': normal((N_C, SB_HEADS * HEAD_DIM, d), (SB_HEADS * HEAD_DIM) ** -0.5),
        'mem_wq': normal((DEPTH, d, mem_w), d ** -0.5),
        'mem_wkv': normal((DEPTH, d, 2 * mem_w), d ** -0.5),
        'mem_wo': normal((DEPTH, mem_w, d), mem_w ** -0.5),
        'ffn_w_in': normal((DEPTH, d, 2 * D_FF), d ** -0.5),
        'ffn_conv_w': normal((DEPTH, FFN_CONV, D_FF), FFN_CONV ** -0.5),
        'ffn_conv_b': normal((DEPTH, D_FF), 0.02),
        'ffn_w_out': normal((DEPTH, D_FF, d), D_FF ** -0.5),
    }


def reference(x_prompt, x_sample, mem_prompt, cache_swa_k, cache_swa_v, state_ssm, state_ssm_conv,
              cache_sb_k, cache_sb_v, cache_mem_k, cache_mem_v, state_ffn_conv, page_table,
              norm_mix_w, norm_xattn_w, norm_ffn_w, final_norm_w, mem_token_norm_w,
              swa_wqkv, swa_sinks, swa_wo,
              ssm_w_in, ssm_conv_w, ssm_conv_b, ssm_dt_bias, ssm_a_log, ssm_d, ssm_norm_w, ssm_w_out,
              sb_wqkv, sb_logit_bias, sb_wo, mem_wq, mem_wkv, mem_wo,
              ffn_w_in, ffn_conv_w, ffn_conv_b, ffn_w_out):
    xp, xs = x_prompt, x_sample
    bp = xp.shape[0]
    mem_n = rmsnorm(mem_prompt, mem_token_norm_w)
    swa_kp, swa_vp, swa_ks, swa_vs = [], [], [], []
    ssm_sp, ssm_cp, ssm_ss, ssm_cs = [], [], [], []
    sb_kp, sb_vp, sb_ks, sb_vs = [], [], [], []
    mem_kp, mem_vp, ffn_cp, ffn_cs = [], [], [], []
    for i in range(DEPTH):
        kind, j = i % N_MIXERS, i // N_MIXERS
        hp = rmsnorm(xp, norm_mix_w[i])
        hs = rmsnorm(xs, norm_mix_w[i])
        if kind == 0:
            op, kp, vp = swa_prompt(hp, swa_wqkv[j], swa_sinks[j], swa_wo[j])
            os_, ks_, vs_ = swa_sample(hs, cache_swa_k[j], cache_swa_v[j], swa_wqkv[j], swa_sinks[j], swa_wo[j])
            swa_kp.append(kp); swa_vp.append(vp); swa_ks.append(ks_); swa_vs.append(vs_)
        elif kind == 1:
            ssm_w = (ssm_w_in[j], ssm_conv_w[j], ssm_conv_b[j], ssm_dt_bias[j], ssm_a_log[j],
                     ssm_d[j], ssm_norm_w[j], ssm_w_out[j])
            conv0 = jnp.zeros((bp, SSM_CONV - 1, SSM_CONV_DIM), xp.dtype)
            st0 = jnp.zeros((bp, SSM_HEADS, SSM_HEADDIM, SSM_D_STATE), state_ssm.dtype)
            op, cp, sp = mamba_mixer(hp, conv0, st0, *ssm_w)
            os_, cs_, ss_ = mamba_mixer(hs, state_ssm_conv[j], state_ssm[j], *ssm_w)
            ssm_sp.append(sp); ssm_cp.append(cp); ssm_ss.append(ss_); ssm_cs.append(cs_)
        else:
            op, kp, vp = sb_prompt(hp, sb_wqkv[j], sb_logit_bias[j], sb_wo[j])
            os_, ks_, vs_ = sb_sample(hs, cache_sb_k[j], cache_sb_v[j], page_table, sb_wqkv[j], sb_logit_bias[j], sb_wo[j])
            sb_kp.append(kp); sb_vp.append(vp); sb_ks.append(ks_); sb_vs.append(vs_)
        xp = xp + op
        xs = xs + os_
        mk, mv = memory_kv(mem_n, mem_wkv[i])
        mem_kp.append(mk); mem_vp.append(mv)
        xp = xp + memory_attend(rmsnorm(xp, norm_xattn_w[i]), mk, mv, mem_wq[i], mem_wo[i])
        xs = xs + memory_attend(rmsnorm(xs, norm_xattn_w[i]), cache_mem_k[i], cache_mem_v[i], mem_wq[i], mem_wo[i])
        hist0 = jnp.zeros((bp, FFN_CONV - 1, D_FF), xp.dtype)
        fp, fcp = conv_glu_ffn(rmsnorm(xp, norm_ffn_w[i]), hist0, ffn_w_in[i], ffn_conv_w[i], ffn_conv_b[i], ffn_w_out[i])
        fs, fcs = conv_glu_ffn(rmsnorm(xs, norm_ffn_w[i]), state_ffn_conv[i], ffn_w_in[i], ffn_conv_w[i], ffn_conv_b[i], ffn_w_out[i])
        ffn_cp.append(fcp); ffn_cs.append(fcs)
        xp = xp + fp
        xs = xs + fs
    y_prompt = rmsnorm(xp, final_norm_w)
    y_sample = rmsnorm(xs, final_norm_w)
    swa_k_prompt = jnp.stack(swa_kp)
    swa_v_prompt = jnp.stack(swa_vp)
    swa_k_sample = jnp.stack(swa_ks)
    swa_v_sample = jnp.stack(swa_vs)
    ssm_state_prompt = jnp.stack(ssm_sp)
    ssm_conv_prompt = jnp.stack(ssm_cp)
    ssm_state_sample = jnp.stack(ssm_ss)
    ssm_conv_sample = jnp.stack(ssm_cs)
    sb_k_prompt = jnp.stack(sb_kp)
    sb_v_prompt = jnp.stack(sb_vp)
    sb_k_sample = jnp.stack(sb_ks)
    sb_v_sample = jnp.stack(sb_vs)
    mem_k_prompt = jnp.stack(mem_kp)
    mem_v_prompt = jnp.stack(mem_vp)
    ffn_conv_prompt = jnp.stack(ffn_cp)
    ffn_conv_sample = jnp.stack(ffn_cs)
    return (y_prompt, y_sample, swa_k_prompt, swa_v_prompt, swa_k_sample, swa_v_sample,
            ssm_state_prompt, ssm_conv_prompt, ssm_state_sample, ssm_conv_sample,
            sb_k_prompt, sb_v_prompt, sb_k_sample, sb_v_sample,
            mem_k_prompt, mem_v_prompt, ffn_conv_prompt, ffn_conv_sample)
```

```python
import functools
import math

import jax
import jax.numpy as jnp
from jax import lax
from jax.experimental import pallas as pl
from jax.experimental.pallas import tpu as pltpu

F32 = jnp.float32
BF16 = jnp.bfloat16
EPS = 1e-6
HEAD_DIM = 64
LANES = 128
SUBLANES = 8
SLOTS = SUBLANES
NEG = -1e30
VMEM_LIMIT = 56 * 1024 * 1024
NT = (((1,), (1,)), ((), ()))
TN = (((0,), (0,)), ((), ()))


def _params(*sem):
    return pltpu.CompilerParams(dimension_semantics=sem, vmem_limit_bytes=VMEM_LIMIT)


def _tile(n, want):
    if n <= want:
        return n
    t = (want // LANES) * LANES
    while t > LANES and n % t:
        t -= LANES
    assert n % t == 0, (n, want)
    return t


def _dot(a, b):
    return jnp.dot(a, b, preferred_element_type=F32)


def _dotg(a, b, dims):
    return lax.dot_general(a, b, dims, preferred_element_type=F32)


def _hi_lo(v, axis):
    hi = v.astype(BF16).astype(F32)
    return jnp.concatenate([hi, v - hi], axis=axis).astype(BF16)


def _softplus(z):
    return jnp.maximum(z, 0.0) + jnp.log1p(jnp.exp(-jnp.abs(z)))


def _silu(v):
    return v * jax.nn.sigmoid(v)


def _rms(x, w):
    ms = jnp.mean(x * x, axis=-1, keepdims=True)
    return x * lax.rsqrt(ms + EPS) * w


def _rms_linear_kernel(x_ref, nw_ref, w_ref, o_ref, xn_ref):
    @pl.when(pl.program_id(1) == 0)
    def _():
        xn_ref[...] = _rms(x_ref[...], nw_ref[...]).astype(BF16)

    o_ref[...] = _dot(xn_ref[...], w_ref[...]).astype(o_ref.dtype)


def _rms_linear(x, nw, w, out_dtype, *, tm=512, tn=512, name="rms_linear"):
    n, d = x.shape
    f = w.shape[1]
    tm, tn = _tile(n, tm), _tile(f, tn)
    return pl.pallas_call(
        _rms_linear_kernel,
        out_shape=jax.ShapeDtypeStruct((n, f), out_dtype),
        grid=(n // tm, f // tn),
        in_specs=[pl.BlockSpec((tm, d), lambda i, j: (i, 0)),
                  pl.BlockSpec((1, d), lambda i, j: (0, 0)),
                  pl.BlockSpec((d, tn), lambda i, j: (0, j))],
        out_specs=pl.BlockSpec((tm, tn), lambda i, j: (i, j)),
        scratch_shapes=[pltpu.VMEM((tm, d), BF16)],
        compiler_params=_params("parallel", "arbitrary"),
        name=name,
    )(x, nw, w)


def _linear_res_kernel(y_ref, w_ref, r_ref, o_ref):
    o_ref[...] = r_ref[...] + _dot(y_ref[...].astype(BF16), w_ref[...])


def _linear_res(y, w, res, *, tm=512, name="linear_res"):
    n, k = y.shape
    d = w.shape[1]
    tm = _tile(n, tm)
    return pl.pallas_call(
        _linear_res_kernel,
        out_shape=jax.ShapeDtypeStruct((n, d), F32),
        grid=(n // tm,),
        in_specs=[pl.BlockSpec((tm, k), lambda i: (i, 0)),
                  pl.BlockSpec((k, d), lambda i: (0, 0)),
                  pl.BlockSpec((tm, d), lambda i: (i, 0))],
        out_specs=pl.BlockSpec((tm, d), lambda i: (i, 0)),
        compiler_params=_params("parallel"),
        name=name,
    )(y, w, res)


def _rmsnorm_kernel(x_ref, nw_ref, o_ref):
    o_ref[...] = _rms(x_ref[...], nw_ref[...])


def _rmsnorm(x, nw, *, tm=512):
    n, d = x.shape
    tm = _tile(n, tm)
    return pl.pallas_call(
        _rmsnorm_kernel,
        out_shape=jax.ShapeDtypeStruct((n, d), F32),
        grid=(n // tm,),
        in_specs=[pl.BlockSpec((tm, d), lambda i: (i, 0)),
                  pl.BlockSpec((1, d), lambda i: (0, 0))],
        out_specs=pl.BlockSpec((tm, d), lambda i: (i, 0)),
        compiler_params=_params("parallel"),
        name="final_rmsnorm",
    )(x, nw)


def _shift_rows(cur, halo, d, row):
    if d == 0:
        return cur
    out = pltpu.roll(cur, d, 0)
    for r in range(d):
        src = SUBLANES - d + r
        out = jnp.where(row == r, halo[src:src + 1], out)
    return out


def _conv_taps(cur, halo, w_ref, b_ref, sl, row):
    k = w_ref.shape[0]
    acc = b_ref[:, sl] + w_ref[k - 1:k, sl] * cur
    for d in range(1, k):
        acc = acc + w_ref[k - 1 - d:k - d, sl] * _shift_rows(cur, halo, d, row)
    return acc


def _conv_taps_slots(ext, w_ref, b_ref, sl):
    k = w_ref.shape[0]
    acc = b_ref[:, sl] + w_ref[k - 1:k, sl] * ext
    for d in range(1, k):
        acc = acc + w_ref[k - 1 - d:k - d, sl] * pltpu.roll(ext, d, 0)
    return acc


def _glu_prompt_kernel(g_ref, u_ref, halo_ref, cw_ref, cb_ref, w_ref, r_ref, o_ref, h_ref,
                       *, tiles_per_seq, lane_chunk):
    tm, f = g_ref.shape
    not_first = (pl.program_id(0) % tiles_per_seq) != 0
    row = lax.broadcasted_iota(jnp.int32, (tm, lane_chunk), 0)
    for c0 in range(0, f, lane_chunk):
        sl = slice(c0, c0 + lane_chunk)
        g = g_ref[:, sl].astype(F32)
        halo = jnp.where(not_first, halo_ref[:, sl].astype(F32), 0.0)
        conv = _conv_taps(g, halo, cw_ref, cb_ref, sl, row)
        h_ref[:, sl] = (_silu(conv) * u_ref[:, sl].astype(F32)).astype(BF16)
    o_ref[...] = r_ref[...] + _dot(h_ref[...], w_ref[...])


def _glu_prompt(gu, cw, cb, w_out, res, seq_len, *, tm=512):
    n, f2 = gu.shape
    f = f2 // 2
    d = w_out.shape[1]
    tm = _tile(seq_len, tm)
    hb = tm // SUBLANES
    lane_chunk = 256 if f % 256 == 0 else LANES
    kern = functools.partial(_glu_prompt_kernel, tiles_per_seq=seq_len // tm, lane_chunk=lane_chunk)
    return pl.pallas_call(
        kern,
        out_shape=jax.ShapeDtypeStruct((n, d), F32),
        grid=(n // tm,),
        in_specs=[pl.BlockSpec((tm, f), lambda i: (i, 0)),
                  pl.BlockSpec((tm, f), lambda i: (i, 1)),
                  pl.BlockSpec((SUBLANES, f), lambda i: (jnp.maximum(i * hb - 1, 0), 0)),
                  pl.BlockSpec(cw.shape, lambda i: (0, 0)),
                  pl.BlockSpec((1, f), lambda i: (0, 0)),
                  pl.BlockSpec((f, d), lambda i: (0, 0)),
                  pl.BlockSpec((tm, d), lambda i: (i, 0))],
        out_specs=pl.BlockSpec((tm, d), lambda i: (i, 0)),
        scratch_shapes=[pltpu.VMEM((tm, f), BF16)],
        compiler_params=_params("parallel"),
        name="glu_prompt",
    )(gu, gu, gu, cw, cb, w_out, res)


def _glu_sample_kernel(g_ref, u_ref, hist_ref, cw_ref, cb_ref, w_ref, r_ref, o_ref, h_ref,
                       *, tok0, lane_chunk):
    tm, f = g_ref.shape
    row = lax.broadcasted_iota(jnp.int32, (tm, lane_chunk), 0)
    is_hist = (row % SLOTS) < tok0
    for c0 in range(0, f, lane_chunk):
        sl = slice(c0, c0 + lane_chunk)
        ext = jnp.where(is_hist, hist_ref[:, sl], g_ref[:, sl].astype(F32))
        conv = _conv_taps_slots(ext, cw_ref, cb_ref, sl)
        h_ref[:, sl] = (_silu(conv) * u_ref[:, sl].astype(F32)).astype(BF16)
    o_ref[...] = r_ref[...] + _dot(h_ref[...], w_ref[...])


def _glu_sample(gu, hist, cw, cb, w_out, res, tok0, *, tm=256):
    n, f2 = gu.shape
    f = f2 // 2
    d = w_out.shape[1]
    tm = _tile(n, tm)
    lane_chunk = 256 if f % 256 == 0 else LANES
    kern = functools.partial(_glu_sample_kernel, tok0=tok0, lane_chunk=lane_chunk)
    return pl.pallas_call(
        kern,
        out_shape=jax.ShapeDtypeStruct((n, d), F32),
        grid=(n // tm,),
        in_specs=[pl.BlockSpec((tm, f), lambda i: (i, 0)),
                  pl.BlockSpec((tm, f), lambda i: (i, 1)),
                  pl.BlockSpec((tm, f), lambda i: (i, 0)),
                  pl.BlockSpec(cw.shape, lambda i: (0, 0)),
                  pl.BlockSpec((1, f), lambda i: (0, 0)),
                  pl.BlockSpec((f, d), lambda i: (0, 0)),
                  pl.BlockSpec((tm, d), lambda i: (i, 0))],
        out_specs=pl.BlockSpec((tm, d), lambda i: (i, 0)),
        scratch_shapes=[pltpu.VMEM((tm, f), BF16)],
        compiler_params=_params("parallel"),
        name="glu_sample",
    )(gu, gu, hist, cw, cb, w_out, res)


def _softmax_pv(s, v):
    m = jnp.max(s, axis=-1, keepdims=True)
    p = jnp.exp(s - m)
    l = jnp.sum(p, axis=-1, keepdims=True)
    return _dot(p.astype(BF16), v) / l


def _mem_heads(q, k, v, n_heads):
    half0 = lax.broadcasted_iota(jnp.int32, (q.shape[0], LANES), 1) < HEAD_DIM
    slabs = []
    for s in range(n_heads // 2):
        ks = k[:, s * LANES:(s + 1) * LANES]
        vs = v[:, s * LANES:(s + 1) * LANES]
        o = [_softmax_pv(_dotg(q[:, (2 * s + c) * LANES:(2 * s + c + 1) * LANES], ks, NT), vs)
             for c in range(2)]
        slabs.append(jnp.where(half0, o[0], o[1]))
    return jnp.concatenate(slabs, axis=1)


def _mem_attn_prompt_kernel(x_ref, nw_ref, wq_ref, k_ref, v_ref, wo_ref, o_ref, *, n_heads):
    x = x_ref[...]
    q = _dot(_rms(x, nw_ref[...]).astype(BF16), wq_ref[...]).astype(BF16)
    o = _mem_heads(q, k_ref[...], v_ref[...], n_heads)
    o_ref[...] = x + _dot(o.astype(BF16), wo_ref[...])


def _mem_attn_prompt(x, nw, wq, k, v, wo, seq_len, n_mem, *, tm=512):
    n, d = x.shape
    hw = k.shape[1]
    tm = _tile(seq_len, tm)
    tps = seq_len // tm
    kern = functools.partial(_mem_attn_prompt_kernel, n_heads=hw // HEAD_DIM)
    return pl.pallas_call(
        kern,
        out_shape=jax.ShapeDtypeStruct((n, d), F32),
        grid=(n // tm,),
        in_specs=[pl.BlockSpec((tm, d), lambda i: (i, 0)),
                  pl.BlockSpec((1, d), lambda i: (0, 0)),
                  pl.BlockSpec(wq.shape, lambda i: (0, 0)),
                  pl.BlockSpec((n_mem, hw), lambda i: (i // tps, 0)),
                  pl.BlockSpec((n_mem, hw), lambda i: (i // tps, 0)),
                  pl.BlockSpec(wo.shape, lambda i: (0, 0))],
        out_specs=pl.BlockSpec((tm, d), lambda i: (i, 0)),
        compiler_params=_params("parallel"),
        name="mem_attn_prompt",
    )(x, nw, wq, k, v, wo)


def _mem_attn_sample_kernel(x_ref, nw_ref, wq_ref, k_ref, v_ref, wo_ref, o_ref, *, n_heads, bb):
    x = x_ref[...]
    q = _dot(_rms(x, nw_ref[...]).astype(BF16), wq_ref[...])
    outs = []
    for b in range(bb):
        qb = q[b * SLOTS:(b + 1) * SLOTS].astype(BF16)
        outs.append(_mem_heads(qb, k_ref[b].astype(BF16), v_ref[b].astype(BF16), n_heads))
    o = jnp.concatenate(outs, axis=0)
    o_ref[...] = x + _dot(o.astype(BF16), wo_ref[...])


def _mem_attn_sample(x, nw, wq, k, v, wo, *, bb=8):
    n, d = x.shape
    bs, n_mem, hw = k.shape
    bb = min(bb, bs)
    kern = functools.partial(_mem_attn_sample_kernel, n_heads=hw // HEAD_DIM, bb=bb)
    return pl.pallas_call(
        kern,
        out_shape=jax.ShapeDtypeStruct((n, d), F32),
        grid=(bs // bb,),
        in_specs=[pl.BlockSpec((bb * SLOTS, d), lambda i: (i, 0)),
                  pl.BlockSpec((1, d), lambda i: (0, 0)),
                  pl.BlockSpec(wq.shape, lambda i: (0, 0)),
                  pl.BlockSpec((bb, n_mem, hw), lambda i: (i, 0, 0)),
                  pl.BlockSpec((bb, n_mem, hw), lambda i: (i, 0, 0)),
                  pl.BlockSpec(wo.shape, lambda i: (0, 0))],
        out_specs=pl.BlockSpec((bb * SLOTS, d), lambda i: (i, 0)),
        compiler_params=_params("parallel"),
        name="mem_attn_sample",
    )(x, nw, wq, k, v, wo)


def _sink_softmax_pv(s, sink, v):
    m = jnp.maximum(jnp.max(s, axis=-1, keepdims=True), sink)
    p = jnp.exp(s - m)
    l = jnp.sum(p, axis=-1, keepdims=True) + jnp.exp(sink - m)
    return _dot(p.astype(BF16), v) / l


def _swa_prompt_kernel(sm_ref, q_ref, kc_ref, vc_ref, kp_ref, vp_ref, o_ref, *, n_heads, group):
    w = q_ref.shape[0]
    n = pl.program_id(1)
    a = lax.broadcasted_iota(jnp.int32, (w, 2 * w), 0)
    j = lax.broadcasted_iota(jnp.int32, (w, 2 * w), 1)
    dist = (a + w - j).astype(F32)
    has_prev = jnp.where(n > 0, 0, w)
    bias_mask = jnp.where((j >= a) & (j <= a + w) & (j >= has_prev), 0.0, NEG)
    half0 = lax.broadcasted_iota(jnp.int32, (w, LANES), 1) < HEAD_DIM
    kv_pairs = n_heads // group // 2
    for p in range(kv_pairs):
        sl = slice(p * LANES, (p + 1) * LANES)
        kpair = jnp.concatenate([kp_ref[:, sl], kc_ref[:, sl]], axis=0)
        vpair = jnp.concatenate([vp_ref[:, sl], vc_ref[:, sl]], axis=0)
        for r in range(group):
            o = []
            for c in range(2):
                h = (2 * p + c) * group + r
                s = _dotg(q_ref[:, h * LANES:(h + 1) * LANES], kpair, NT)
                s = s - sm_ref[0, h] * dist + bias_mask
                o.append(_sink_softmax_pv(s, sm_ref[1, h], vpair))
            slab = p * group + r
            o_ref[:, slab * LANES:(slab + 1) * LANES] = jnp.where(half0, o[0], o[1]).astype(BF16)


def _swa_prompt(qkv, sm, batch, seq_len, window, n_heads, group):
    n = qkv.shape[0]
    nb = seq_len // window
    qw = n_heads * LANES
    kvw = n_heads // group * HEAD_DIM
    kcol, vcol = qw // kvw, qw // kvw + 1
    kern = functools.partial(_swa_prompt_kernel, n_heads=n_heads, group=group)
    cur = lambda b, i: b * nb + i
    prev = lambda b, i: b * nb + jnp.maximum(i - 1, 0)
    return pl.pallas_call(
        kern,
        out_shape=jax.ShapeDtypeStruct((n, n_heads * HEAD_DIM), BF16),
        grid=(batch, nb),
        in_specs=[pl.BlockSpec(memory_space=pltpu.SMEM),
                  pl.BlockSpec((window, qw), lambda b, i: (cur(b, i), 0)),
                  pl.BlockSpec((window, kvw), lambda b, i: (cur(b, i), kcol)),
                  pl.BlockSpec((window, kvw), lambda b, i: (cur(b, i), vcol)),
                  pl.BlockSpec((window, kvw), lambda b, i: (prev(b, i), kcol)),
                  pl.BlockSpec((window, kvw), lambda b, i: (prev(b, i), vcol))],
        out_specs=pl.BlockSpec((window, n_heads * HEAD_DIM), lambda b, i: (cur(b, i), 0)),
        compiler_params=_params("parallel", "parallel"),
        name="swa_prompt",
    )(sm, qkv, qkv, qkv, qkv, qkv)


def _swa_sample_kernel(sm_ref, q_ref, kn_ref, vn_ref, kc_ref, vc_ref, o_ref,
                       *, n_heads, group, bb, tok0):
    w = kc_ref.shape[1]
    rows = group * SLOTS
    row = lax.broadcasted_iota(jnp.int32, (rows, 2 * w), 0)
    j = lax.broadcasted_iota(jnp.int32, (rows, 2 * w), 1)
    tq = row % SLOTS - tok0
    in_cache = j < w
    dist_i = jnp.where(in_cache, w + tq - j, tq + tok0 + w - j)
    mask_cache = jnp.where(j >= tq, 0.0, NEG)
    mask_new = jnp.where((j >= w + tok0) & (j < w + SLOTS) & (dist_i >= 0), 0.0, NEG)
    bias_mask = jnp.where(in_cache, mask_cache, mask_new)
    dist = dist_i.astype(F32)
    rgrp = lax.broadcasted_iota(jnp.int32, (rows, 1), 0) // SLOTS
    half0 = lax.broadcasted_iota(jnp.int32, (SLOTS, LANES), 1) < HEAD_DIM
    zeros = jnp.zeros((w - SLOTS, LANES), F32)
    q_all = q_ref[...].astype(F32)
    kn_all = kn_ref[...].astype(F32)
    vn_all = vn_ref[...].astype(F32)
    kv_pairs = n_heads // group // 2
    out_rows = []
    for b in range(bb):
        rs = slice(b * SLOTS, (b + 1) * SLOTS)
        slabs = [None] * (n_heads // 2)
        for p in range(kv_pairs):
            sl = slice(p * LANES, (p + 1) * LANES)
            kfull = jnp.concatenate([kc_ref[b, :, sl], kn_all[rs, sl], zeros], axis=0).astype(BF16)
            vfull = jnp.concatenate([vc_ref[b, :, sl], vn_all[rs, sl], zeros], axis=0).astype(BF16)
            o = []
            for c in range(2):
                heads = [(2 * p + c) * group + r for r in range(group)]
                qs = jnp.concatenate([q_all[rs, h * LANES:(h + 1) * LANES] for h in heads], axis=0)
                slope = jnp.zeros((rows, 1), F32)
                sink = jnp.zeros((rows, 1), F32)
                for r, h in enumerate(heads):
                    slope = jnp.where(rgrp == r, sm_ref[0, h], slope)
                    sink = jnp.where(rgrp == r, sm_ref[1, h], sink)
                s = _dotg(qs.astype(BF16), kfull, NT) - slope * dist + bias_mask
                o.append(_sink_softmax_pv(s, sink, vfull))
            for r in range(group):
                piece = slice(r * SLOTS, (r + 1) * SLOTS)
                slabs[p * group + r] = jnp.where(half0, o[0][piece], o[1][piece])
        out_rows.append(jnp.concatenate(slabs, axis=1))
    o_ref[...] = jnp.concatenate(out_rows, axis=0).astype(BF16)


def _swa_sample(qkv, sm, kc, vc, n_heads, group, tok0, *, bb=8):
    n = qkv.shape[0]
    bs, window, kvw = kc.shape
    bb = min(bb, bs)
    qw = n_heads * LANES
    kcol, vcol = qw // kvw, qw // kvw + 1
    kern = functools.partial(_swa_sample_kernel, n_heads=n_heads, group=group, bb=bb, tok0=tok0)
    return pl.pallas_call(
        kern,
        out_shape=jax.ShapeDtypeStruct((n, n_heads * HEAD_DIM), BF16),
        grid=(bs // bb,),
        in_specs=[pl.BlockSpec(memory_space=pltpu.SMEM),
                  pl.BlockSpec((bb * SLOTS, qw), lambda i: (i, 0)),
                  pl.BlockSpec((bb * SLOTS, kvw), lambda i: (i, kcol)),
                  pl.BlockSpec((bb * SLOTS, kvw), lambda i: (i, vcol)),
                  pl.BlockSpec((bb, window, kvw), lambda i: (i, 0, 0)),
                  pl.BlockSpec((bb, window, kvw), lambda i: (i, 0, 0))],
        out_specs=pl.BlockSpec((bb * SLOTS, n_heads * HEAD_DIM), lambda i: (i, 0)),
        compiler_params=_params("parallel"),
        name="swa_sample",
    )(sm, qkv, qkv, qkv, kc, vc)


def _sb_block(q, k, v, bias, uu, r_prev, valid):
    kb = k.shape[0]
    z = _dotg(q, k, NT) + bias
    lk = -_softplus(z)
    if valid is not None:
        lk = jnp.where(valid, lk, 0.0)
    both = _dot(_hi_lo(lk, 1), uu)
    e = z + both[:, :kb] + r_prev
    if valid is not None:
        e = jnp.where(valid, e, NEG)
    return _dot(jnp.exp(e).astype(BF16), v), both[:, kb:]


def _sb_prompt_kernel(bias_ref, q_ref, k_ref, v_ref, uu_ref, o_ref, acc_ref, r_ref):
    blk = q_ref.shape[0]
    pair = pl.program_id(1)
    i = pl.program_id(2)
    t = lax.broadcasted_iota(jnp.int32, (blk, blk), 0)
    s = lax.broadcasted_iota(jnp.int32, (blk, blk), 1)
    strictly_before = s < t

    def visit(j, valid):
        start = pl.multiple_of(j * blk, blk)
        kj = k_ref[pl.ds(start, blk), :]
        vj = v_ref[pl.ds(start, blk), :]
        for c in range(2):
            pv, tot = _sb_block(q_ref[:, c * LANES:(c + 1) * LANES], kj, vj,
                                bias_ref[2 * pair + c], uu_ref[...], r_ref[c], valid)
            acc_ref[c] = acc_ref[c] + pv
            r_ref[c] = r_ref[c] + tot

    acc_ref[...] = jnp.zeros_like(acc_ref)
    r_ref[...] = jnp.zeros_like(r_ref)
    visit(i, strictly_before)

    def body(jj, carry):
        visit(i - 1 - jj, None)
        return carry

    lax.fori_loop(0, i, body, 0)
    half0 = lax.broadcasted_iota(jnp.int32, (blk, LANES), 1) < HEAD_DIM
    o_ref[...] = jnp.where(half0, acc_ref[0], acc_ref[1]).astype(BF16)


def _sb_prompt(qkv, bias, uu, batch, seq_len, n_heads, *, blk=128):
    n = qkv.shape[0]
    nq = seq_len // blk
    pairs = n_heads // 2
    kcol0 = n_heads
    vcol0 = n_heads + pairs
    return pl.pallas_call(
        _sb_prompt_kernel,
        out_shape=jax.ShapeDtypeStruct((n, n_heads * HEAD_DIM), BF16),
        grid=(batch, pairs, nq),
        in_specs=[pl.BlockSpec(memory_space=pltpu.SMEM),
                  pl.BlockSpec((blk, 2 * LANES), lambda b, p, i: (b * nq + i, p)),
                  pl.BlockSpec((seq_len, LANES), lambda b, p, i: (b, kcol0 + p)),
                  pl.BlockSpec((seq_len, LANES), lambda b, p, i: (b, vcol0 + p)),
                  pl.BlockSpec(uu.shape, lambda b, p, i: (0, 0))],
        out_specs=pl.BlockSpec((blk, LANES), lambda b, p, i: (b * nq + i, p)),
        scratch_shapes=[pltpu.VMEM((2, blk, LANES), F32), pltpu.VMEM((2, blk, LANES), F32)],
        compiler_params=_params("parallel", "parallel", "arbitrary"),
        name="sb_prompt",
    )(bias, qkv, qkv, qkv, uu)


def _sb_sample_kernel(pt_ref, q_ref, kn_ref, vn_ref, bias_ref, mask_ref, uu_ref, *rest,
                      n_heads, n_tok, tok0, pages_per_step):
    k_refs = rest[:pages_per_step]
    v_refs = rest[pages_per_step:2 * pages_per_step]
    o_ref, qb_ref, acc_ref, r_ref = rest[2 * pages_per_step:]
    ps = k_refs[0].shape[0]
    step = pl.program_id(1)
    rows = n_tok * n_heads

    def visit(k, v, valid):
        pv, tot = _sb_block(qb_ref[...], k, v, bias_ref[...], uu_ref[...], r_ref[...], valid)
        acc_ref[...] = acc_ref[...] + pv
        r_ref[...] = r_ref[...] + tot

    @pl.when(step == 0)
    def _():
        q = q_ref[...]
        for t in range(n_tok):
            qb_ref[t * n_heads:(t + 1) * n_heads, :] = (
                q[tok0 + t:tok0 + t + 1, :] * mask_ref[...]).astype(BF16)
        acc_ref[...] = jnp.zeros_like(acc_ref)
        r_ref[...] = jnp.zeros_like(r_ref)
        zeros = jnp.zeros((ps - SLOTS, kn_ref.shape[1]), F32)
        kn = jnp.concatenate([kn_ref[...], zeros], axis=0).astype(BF16)
        vn = jnp.concatenate([vn_ref[...], zeros], axis=0).astype(BF16)
        tq = lax.broadcasted_iota(jnp.int32, (rows, ps), 0) // n_heads
        j = lax.broadcasted_iota(jnp.int32, (rows, ps), 1)
        visit(kn, vn, (j >= tok0) & (j < tok0 + tq))

    for g in range(pages_per_step):
        visit(k_refs[g][...].astype(BF16), v_refs[g][...].astype(BF16), None)

    @pl.when(step == pl.num_programs(1) - 1)
    def _():
        o_ref[0:tok0, :] = jnp.zeros((tok0, o_ref.shape[1]), o_ref.dtype)
        for t in range(n_tok):
            blk = acc_ref[t * n_heads:(t + 1) * n_heads, :] * mask_ref[...]
            o_ref[tok0 + t:tok0 + t + 1, :] = jnp.sum(blk, axis=0, keepdims=True).astype(o_ref.dtype)


def _sb_sample(qkv, pool_k, pool_v, page_table, bias_col, head_mask, uu, n_heads, n_tok, tok0,
               *, pages_per_step=8):
    n = qkv.shape[0]
    bs, n_pages = page_table.shape
    _, ps, hw = pool_k.shape
    g = pages_per_step
    while n_pages % g:
        g //= 2
    steps = n_pages // g
    rows = n_tok * n_heads
    pt = page_table.reshape(-1)

    def page_map(gi):
        return lambda b, s, pt_ref: (pt_ref[b * n_pages + n_pages - 1 - (s * g + gi)], 0, 0)

    page_specs = [pl.BlockSpec((None, ps, hw), page_map(gi)) for gi in range(g)]
    kern = functools.partial(_sb_sample_kernel, n_heads=n_heads, n_tok=n_tok, tok0=tok0,
                             pages_per_step=g)
    grid_spec = pltpu.PrefetchScalarGridSpec(
        num_scalar_prefetch=1,
        grid=(bs, steps),
        in_specs=[pl.BlockSpec((SLOTS, hw), lambda b, s, pt_ref: (b, 0)),
                  pl.BlockSpec((SLOTS, hw), lambda b, s, pt_ref: (b, 1)),
                  pl.BlockSpec((SLOTS, hw), lambda b, s, pt_ref: (b, 2)),
                  pl.BlockSpec((rows, 1), lambda b, s, pt_ref: (0, 0)),
                  pl.BlockSpec((n_heads, hw), lambda b, s, pt_ref: (0, 0)),
                  pl.BlockSpec(uu.shape, lambda b, s, pt_ref: (0, 0))] + page_specs + page_specs,
        out_specs=pl.BlockSpec((SLOTS, hw), lambda b, s, pt_ref: (b, 0)),
        scratch_shapes=[pltpu.VMEM((rows, hw), BF16), pltpu.VMEM((rows, hw), F32),
                        pltpu.VMEM((rows, ps), F32)],
    )
    return pl.pallas_call(
        kern,
        out_shape=jax.ShapeDtypeStruct((n, hw), F32),
        grid_spec=grid_spec,
        compiler_params=_params("parallel", "arbitrary"),
        name="sb_sample",
    )(pt, qkv, qkv, qkv, bias_col, head_mask, uu, *([pool_k] * g), *([pool_v] * g))


def _expand_heads(v, ex2):
    return _dot(_hi_lo(v, 1), ex2)


def _ssd_diag(acs, cb, xs, same_seq, half_masks, h0):
    acs_t = acs.T
    y = None
    for c in range(2):
        h = h0 + c
        seg = acs[:, h:h + 1] - acs_t[h:h + 1, :]
        m = (cb * jnp.exp(jnp.where(same_seq, seg, NEG))).astype(BF16)
        term = _dot(m, jnp.where(half_masks[c], xs, 0.0).astype(BF16))
        y = term if y is None else y + term
    return y


def _ssd_prompt_kernel(z_ref, x_ref, bc_ref, xh_ref, bch_ref, dt_ref, cwx_ref, cbx_ref, cwb_ref,
                       cbb_ref, dtb_ref, a_ref, d_ref, nw_ref, tt_ref, ex_ref, yg_ref, st_ref,
                       state_ref, *, groups, d_state, lane_chunk):
    lc, di = x_ref.shape
    c_idx = pl.program_id(1)
    not_first = c_idx > 0

    @pl.when(c_idx == 0)
    def _():
        state_ref[...] = jnp.zeros_like(state_ref)

    row = lax.broadcasted_iota(jnp.int32, (lc, lane_chunk), 0)

    def conv_silu(cur_ref, halo_ref, w_ref, b_ref):
        outs = []
        for c0 in range(0, cur_ref.shape[1], lane_chunk):
            sl = slice(c0, c0 + lane_chunk)
            halo = jnp.where(not_first, halo_ref[:, sl].astype(F32), 0.0)
            outs.append(_silu(_conv_taps(cur_ref[:, sl].astype(F32), halo, w_ref, b_ref, sl, row)))
        return jnp.concatenate(outs, axis=1)

    xc = conv_silu(x_ref, xh_ref, cwx_ref, cbx_ref)
    bcc = conv_silu(bc_ref, bch_ref, cwb_ref, cbb_ref)
    gn = groups * d_state
    dt = _softplus(dt_ref[...] + dtb_ref[...])
    a = dt * a_ref[...]
    acs = _dot(tt_ref[...], _hi_lo(a, 0))
    dt_full = _expand_heads(dt, ex_ref[...])
    acs_full = _expand_heads(acs, ex_ref[...])
    eacs_full = jnp.exp(acs_full)
    te_full = jnp.exp(_expand_heads(acs[lc - 1:lc, :] - acs, ex_ref[...]))
    cd_full = eacs_full[lc - 1:lc, :]
    xdt = xc * dt_full
    ti = lax.broadcasted_iota(jnp.int32, (lc, lc), 0)
    si = lax.broadcasted_iota(jnp.int32, (lc, lc), 1)
    causal = si <= ti
    lane = lax.broadcasted_iota(jnp.int32, (lc, LANES), 1)
    half_masks = (lane < HEAD_DIM, lane >= HEAD_DIM)
    pairs_per_group = di // LANES // groups
    ys = []
    for g in range(groups):
        bg = bcc[:, g * d_state:(g + 1) * d_state]
        cg = bcc[:, gn + g * d_state:gn + (g + 1) * d_state].astype(BF16)
        cb = _dotg(cg, bg.astype(BF16), NT)
        bg_t = bg.T.astype(BF16)
        for pp in range(pairs_per_group):
            p = g * pairs_per_group + pp
            sl = slice(p * LANES, (p + 1) * LANES)
            xs = xdt[:, sl]
            st = state_ref[p]
            y = _ssd_diag(acs, cb, xs, causal, half_masks, 2 * p)
            y = y + _dot(cg, st.astype(BF16)) * eacs_full[:, sl] + xc[:, sl] * d_ref[:, sl]
            state_ref[p] = st * cd_full[:, sl] + _dot(bg_t, (xs * te_full[:, sl]).astype(BF16))
            ys.append(y)
    y = jnp.concatenate(ys, axis=1)
    yz = y * _silu(z_ref[...].astype(F32))
    yg_ref[...] = _rms(yz, nw_ref[...]).astype(BF16)

    @pl.when(c_idx == pl.num_programs(1) - 1)
    def _():
        st_ref[0] = state_ref[...]


def _ssd_prompt(zx, dt, cwx, cbx, cwb, cbb, dtb, a, dfull, nw, tt, ex2, batch, seq_len, chunk,
                groups, d_state):
    n = zx.shape[0]
    di = dfull.shape[1]
    nc = seq_len // chunk
    hb = chunk // SUBLANES
    pairs = di // LANES
    kern = functools.partial(_ssd_prompt_kernel, groups=groups, d_state=d_state, lane_chunk=512)
    cur = lambda b, c: b * nc + c
    halo = lambda b, c: jnp.maximum((b * nc + c) * hb - 1, 0)
    full = lambda arr: pl.BlockSpec(arr.shape, lambda b, c: (0, 0))
    return pl.pallas_call(
        kern,
        out_shape=(jax.ShapeDtypeStruct((n, di), BF16),
                   jax.ShapeDtypeStruct((batch, pairs, d_state, LANES), F32)),
        grid=(batch, nc),
        in_specs=[pl.BlockSpec((chunk, di), lambda b, c: (cur(b, c), 0)),
                  pl.BlockSpec((chunk, di), lambda b, c: (cur(b, c), 1)),
                  pl.BlockSpec((chunk, di), lambda b, c: (cur(b, c), 2)),
                  pl.BlockSpec((SUBLANES, di), lambda b, c: (halo(b, c), 1)),
                  pl.BlockSpec((SUBLANES, di), lambda b, c: (halo(b, c), 2)),
                  pl.BlockSpec((chunk, LANES), lambda b, c: (cur(b, c), 0)),
                  full(cwx), full(cbx), full(cwb), full(cbb), full(dtb), full(a), full(dfull),
                  full(nw), full(tt), full(ex2)],
        out_specs=(pl.BlockSpec((chunk, di), lambda b, c: (cur(b, c), 0)),
                   pl.BlockSpec((1, pairs, d_state, LANES), lambda b, c: (b, 0, 0, 0))),
        scratch_shapes=[pltpu.VMEM((pairs, d_state, LANES), F32)],
        compiler_params=_params("parallel", "arbitrary"),
        name="ssd_prompt",
    )(zx, zx, zx, zx, zx, dt, cwx, cbx, cwb, cbb, dtb, a, dfull, nw, tt, ex2)


def _ssd_sample_kernel(z_ref, x_ref, bc_ref, hx_ref, hbc_ref, dt_ref, cwx_ref, cbx_ref, cwb_ref,
                       cbb_ref, dtb_ref, a_ref, d_ref, nw_ref, ex_ref, sel_ref, st_in_ref,
                       yg_ref, st_out_ref, c_s, b_s, xw_s, ea_s, y_s,
                       *, groups, d_state, tok0, lane_chunk):
    rows, di = x_ref.shape
    bi = pl.program_id(1)
    gn = groups * d_state
    pairs_per_group = di // LANES // groups

    @pl.when(bi == 0)
    def _():
        slot_c = lax.broadcasted_iota(jnp.int32, (rows, lane_chunk), 0) % SLOTS

        def conv_silu(cur_ref, hist_ref, w_ref, b_ref):
            outs = []
            for c0 in range(0, cur_ref.shape[1], lane_chunk):
                sl = slice(c0, c0 + lane_chunk)
                ext = jnp.where(slot_c < tok0, hist_ref[:, sl], cur_ref[:, sl].astype(F32))
                outs.append(_silu(_conv_taps_slots(ext, w_ref, b_ref, sl)))
            return jnp.concatenate(outs, axis=1)

        xc = conv_silu(x_ref, hx_ref, cwx_ref, cbx_ref)
        bcc = conv_silu(bc_ref, hbc_ref, cwb_ref, cbb_ref)
        slot = lax.broadcasted_iota(jnp.int32, (rows, LANES), 0) % SLOTS
        dt = jnp.where(slot >= tok0, _softplus(dt_ref[...] + dtb_ref[...]), 0.0)
        a = dt * a_ref[...]
        acs, rev = a, jnp.zeros_like(a)
        for d in range(1, SLOTS - tok0):
            acs = acs + jnp.where(slot - d >= tok0, pltpu.roll(a, d, 0), 0.0)
            rev = rev + jnp.where(slot + d < SLOTS, pltpu.roll(a, rows - d, 0), 0.0)
        dt_full = _expand_heads(dt, ex_ref[...])
        acs_full = _expand_heads(acs, ex_ref[...])
        te_full = jnp.exp(_expand_heads(rev, ex_ref[...]))
        xdt = xc * dt_full
        ti = lax.broadcasted_iota(jnp.int32, (rows, rows), 0)
        si = lax.broadcasted_iota(jnp.int32, (rows, rows), 1)
        same_seq = (si <= ti) & (si // SLOTS == ti // SLOTS)
        lane = lax.broadcasted_iota(jnp.int32, (rows, LANES), 1)
        half_masks = (lane < HEAD_DIM, lane >= HEAD_DIM)
        ys = []
        for g in range(groups):
            bg = bcc[:, g * d_state:(g + 1) * d_state].astype(BF16)
            cg = bcc[:, gn + g * d_state:gn + (g + 1) * d_state].astype(BF16)
            cb = _dotg(cg, bg, NT)
            for pp in range(pairs_per_group):
                p = g * pairs_per_group + pp
                sl = slice(p * LANES, (p + 1) * LANES)
                ys.append(_ssd_diag(acs, cb, xdt[:, sl], same_seq, half_masks, 2 * p)
                          + xc[:, sl] * d_ref[:, sl])
        y_s[...] = jnp.concatenate(ys, axis=1)
        b_s[...] = bcc[:, :gn]
        c_s[...] = bcc[:, gn:]
        xw_s[...] = xdt * te_full
        ea_s[...] = jnp.exp(acs_full)

    r0 = pl.multiple_of(bi * SLOTS, SLOTS)
    pad = jnp.zeros((SLOTS, LANES), F32)
    cb_rows = c_s[pl.ds(r0, SLOTS), :]
    bb_rows = b_s[pl.ds(r0, SLOTS), :]
    xw_rows = xw_s[pl.ds(r0, SLOTS), :]
    ea_rows = ea_s[pl.ds(r0, SLOTS), :]
    y_off = []
    for g in range(groups):
        gs = slice(g * d_state, (g + 1) * d_state)
        cg = jnp.concatenate([cb_rows[:, gs], pad], axis=0).astype(BF16)
        bg = jnp.concatenate([bb_rows[:, gs], pad], axis=0).astype(BF16)
        for pp in range(pairs_per_group):
            p = g * pairs_per_group + pp
            sl = slice(p * LANES, (p + 1) * LANES)
            st = st_in_ref[0, p]
            y_off.append(_dotg(cg, st.astype(BF16), NT)[:SLOTS] * ea_rows[:, sl])
            decay = _dotg(_hi_lo(ea_rows[:, sl], 0), sel_ref[...], TN)
            xw = jnp.concatenate([xw_rows[:, sl], pad], axis=0).astype(BF16)
            st_out_ref[0, p] = st * decay + _dotg(xw, bg, TN)
    y_s[pl.ds(r0, SLOTS), :] = y_s[pl.ds(r0, SLOTS), :] + jnp.concatenate(y_off, axis=1)

    @pl.when(bi == pl.num_programs(1) - 1)
    def _():
        yz = y_s[...] * _silu(z_ref[...].astype(F32))
        yg_ref[...] = _rms(yz, nw_ref[...]).astype(BF16)


def _ssd_sample(zx, hist, dt, cwx, cbx, cwb, cbb, dtb, a, dfull, nw, ex2, sel, state, groups,
                d_state, tok0, *, bb=16):
    n = zx.shape[0]
    bs = state.shape[0]
    di = dfull.shape[1]
    pairs = di // LANES
    bb = min(bb, bs)
    rows = bb * SLOTS
    gn = groups * d_state
    kern = functools.partial(_ssd_sample_kernel, groups=groups, d_state=d_state, tok0=tok0,
                             lane_chunk=512)
    full = lambda arr: pl.BlockSpec(arr.shape, lambda i, j: (0, 0))
    st_spec = pl.BlockSpec((1, pairs, LANES, d_state), lambda i, j: (i * bb + j, 0, 0, 0))
    return pl.pallas_call(
        kern,
        out_shape=(jax.ShapeDtypeStruct((n, di), BF16),
                   jax.ShapeDtypeStruct(state.shape, F32)),
        grid=(bs // bb, bb),
        in_specs=[pl.BlockSpec((rows, di), lambda i, j: (i, 0)),
                  pl.BlockSpec((rows, di), lambda i, j: (i, 1)),
                  pl.BlockSpec((rows, di), lambda i, j: (i, 2)),
                  pl.BlockSpec((rows, di), lambda i, j: (i, 0)),
                  pl.BlockSpec((rows, di), lambda i, j: (i, 1)),
                  pl.BlockSpec((rows, LANES), lambda i, j: (i, 0)),
                  full(cwx), full(cbx), full(cwb), full(cbb), full(dtb), full(a), full(dfull),
                  full(nw), full(ex2), full(sel), st_spec],
        out_specs=(pl.BlockSpec((rows, di), lambda i, j: (i, 0)), st_spec),
        scratch_shapes=[pltpu.VMEM((rows, gn), F32), pltpu.VMEM((rows, gn), F32),
                        pltpu.VMEM((rows, di), F32), pltpu.VMEM((rows, di), F32),
                        pltpu.VMEM((rows, di), F32)],
        compiler_params=_params("parallel", "arbitrary"),
        name="ssd_sample",
    )(zx, zx, zx, hist, hist, dt, cwx, cbx, cwb, cbb, dtb, a, dfull, nw, ex2, sel, state)


def _widen_heads(w, halves):
    d = w.shape[0]
    h = len(halves)
    onehot = jax.nn.one_hot(jnp.asarray(halves), 2, dtype=w.dtype)
    wide = w.reshape(d, h, 1, HEAD_DIM) * onehot[None, :, :, None]
    return wide.reshape(d, h * LANES)


def _suffix_sum_matrix(n):
    j = jnp.arange(n)[:, None]
    s = jnp.arange(n)[None, :]
    u = jnp.concatenate([(j >= s).astype(F32), jnp.ones((n, n), F32)], axis=1)
    return jnp.concatenate([u, u], axis=0).astype(BF16)


def _pad_slots(a, lead):
    k = a.shape[-2]
    pad = [(0, 0)] * (a.ndim - 2) + [(lead, SLOTS - lead - k), (0, 0)]
    a = jnp.pad(a, pad)
    return a.reshape(a.shape[:-3] + (a.shape[-3] * SLOTS, a.shape[-1]))


def kernel(x_prompt, x_sample, mem_prompt, cache_swa_k, cache_swa_v, state_ssm, state_ssm_conv, cache_sb_k, cache_sb_v, cache_mem_k, cache_mem_v, state_ffn_conv, page_table, norm_mix_w, norm_xattn_w, norm_ffn_w, final_norm_w, mem_token_norm_w, swa_wqkv, swa_sinks, swa_wo, ssm_w_in, ssm_conv_w, ssm_conv_b, ssm_dt_bias, ssm_a_log, ssm_d, ssm_norm_w, ssm_w_out, sb_wqkv, sb_logit_bias, sb_wo, mem_wq, mem_wkv, mem_wo, ffn_w_in, ffn_conv_w, ffn_conv_b, ffn_w_out):
    bp, seq, d = x_prompt.shape
    bs, n_tok, _ = x_sample.shape
    depth = norm_mix_w.shape[0]
    tok0 = SLOTS - n_tok
    scale = 1.0 / math.sqrt(HEAD_DIM)
    row = lambda v: v.reshape(1, -1).astype(F32)

    window, swa_kvh = cache_swa_k.shape[2], cache_swa_k.shape[3]
    swa_heads = swa_sinks.shape[1]
    swa_group = swa_heads // swa_kvh
    swa_kvw = swa_kvh * HEAD_DIM
    ssm_heads, ssm_p, d_state = state_ssm.shape[2], state_ssm.shape[3], state_ssm.shape[4]
    d_inner = ssm_heads * ssm_p
    conv_dim = ssm_conv_w.shape[2]
    gn = (conv_dim - d_inner) // 2
    groups = gn // d_state
    ssm_k = ssm_conv_w.shape[1]
    sb_heads = sb_logit_bias.shape[1]
    sb_w = sb_heads * HEAD_DIM
    page_size = cache_sb_k.shape[2]
    n_mem, mem_heads = cache_mem_k.shape[2], cache_mem_k.shape[3]
    mem_w = mem_heads * HEAD_DIM
    d_ff = ffn_conv_w.shape[2]
    ffn_k = ffn_conv_w.shape[1]
    assert ssm_p == HEAD_DIM and d_state == LANES and ssm_heads <= LANES
    assert n_tok + max(ssm_k, ffn_k) - 1 <= SLOTS
    chunk = math.gcd(seq, 128)

    uu_blk = _suffix_sum_matrix(128)
    uu_page = _suffix_sum_matrix(page_size)
    tri = (jnp.arange(chunk)[None, :] <= jnp.arange(chunk)[:, None]).astype(BF16)
    tt = jnp.concatenate([tri, tri], axis=1)
    ex = (jnp.arange(d_inner)[None, :] // ssm_p == jnp.arange(LANES)[:, None]).astype(BF16)
    ex2 = jnp.concatenate([ex, ex], axis=0)
    sel = jnp.zeros((2 * SLOTS, d_state), F32).at[SLOTS - 1].set(1.0).at[2 * SLOTS - 1].set(1.0)
    sel = sel.astype(BF16)
    sb_head_mask = (jnp.arange(sb_w)[None, :] // HEAD_DIM == jnp.arange(sb_heads)[:, None]).astype(F32)

    xp = x_prompt.reshape(bp * seq, d)
    xs = _pad_slots(x_sample, tok0)

    wkv_all = jnp.transpose(mem_wkv, (1, 0, 2)).reshape(d, depth * 2 * mem_w).astype(BF16)
    mem_kv = _rms_linear(mem_prompt.reshape(bp * n_mem, d), row(mem_token_norm_w), wkv_all, F32,
                         name="mem_kv")
    mem_kv5 = mem_kv.reshape(bp, n_mem, depth, 2, mem_heads, HEAD_DIM)
    mem_k_prompt = jnp.moveaxis(mem_kv5[:, :, :, 0], 2, 0)
    mem_v_prompt = jnp.moveaxis(mem_kv5[:, :, :, 1], 2, 0)
    mem_kv_b = mem_kv.astype(BF16)

    ffn_hist = _pad_slots(state_ffn_conv, tok0 - (ffn_k - 1))

    outs = {k: [] for k in ("swa_kp", "swa_vp", "swa_ks", "swa_vs", "ssm_sp", "ssm_cp", "ssm_ss",
                            "ssm_cs", "sb_kp", "sb_vp", "sb_ks", "sb_vs", "ffn_cp", "ffn_cs")}

    for i in range(depth):
        kind, j = i % 3, i // 3
        nw_mix = row(norm_mix_w[i])
        if kind == 0:
            halves = [(h // swa_group) % 2 for h in range(swa_heads)]
            qw = swa_heads * HEAD_DIM
            w_all = jnp.concatenate([_widen_heads(swa_wqkv[j][:, :qw] * scale, halves),
                                     swa_wqkv[j][:, qw:]], axis=1).astype(BF16)
            order = [(2 * p + c) * swa_group + r for p in range(swa_kvh // 2)
                     for r in range(swa_group) for c in range(2)]
            wo = swa_wo[j].reshape(swa_heads, HEAD_DIM, d)[jnp.asarray(order)].reshape(qw, d).astype(BF16)
            slopes = jnp.exp2(-8.0 * jnp.arange(1, swa_heads + 1, dtype=F32) / swa_heads)
            sm = jnp.stack([slopes, swa_sinks[j].astype(F32)])
            kcol = swa_heads * LANES

            qkv_p = _rms_linear(xp, nw_mix, w_all, BF16, name="swa_qkv_prompt")
            o_p = _swa_prompt(qkv_p, sm, bp, seq, window, swa_heads, swa_group)
            xp = _linear_res(o_p, wo, xp, name="swa_out_prompt")
            kv_p = qkv_p.reshape(bp, seq, -1)[:, seq - window:, kcol:].astype(F32)
            outs["swa_kp"].append(kv_p[..., :swa_kvw].reshape(bp, window, swa_kvh, HEAD_DIM))
            outs["swa_vp"].append(kv_p[..., swa_kvw:].reshape(bp, window, swa_kvh, HEAD_DIM))

            qkv_s = _rms_linear(xs, nw_mix, w_all, BF16, tm=1024, name="swa_qkv_sample")
            o_s = _swa_sample(qkv_s, sm, cache_swa_k[j].reshape(bs, window, swa_kvw),
                              cache_swa_v[j].reshape(bs, window, swa_kvw), swa_heads, swa_group, tok0)
            xs = _linear_res(o_s, wo, xs, name="swa_out_sample")
            kv_s = qkv_s.reshape(bs, SLOTS, -1)[:, tok0:, kcol:].astype(F32)
            k_new = kv_s[..., :swa_kvw].reshape(bs, n_tok, swa_kvh, HEAD_DIM)
            v_new = kv_s[..., swa_kvw:].reshape(bs, n_tok, swa_kvh, HEAD_DIM)
            outs["swa_ks"].append(jnp.concatenate([cache_swa_k[j][:, n_tok:], k_new], axis=1))
            outs["swa_vs"].append(jnp.concatenate([cache_swa_v[j][:, n_tok:], v_new], axis=1))
        elif kind == 1:
            zxw = 2 * d_inner + 2 * gn
            w_main = ssm_w_in[j][:, :zxw].astype(BF16)
            w_dt = jnp.pad(ssm_w_in[j][:, zxw:], ((0, 0), (0, LANES - ssm_heads))).astype(BF16)
            pad_h = lambda v: jnp.pad(v.astype(F32), (0, LANES - ssm_heads)).reshape(1, LANES)
            cwx, cwb = ssm_conv_w[j][:, :d_inner], ssm_conv_w[j][:, d_inner:]
            cbx, cbb = row(ssm_conv_b[j][:d_inner]), row(ssm_conv_b[j][d_inner:])
            dtb = pad_h(ssm_dt_bias[j])
            a_neg = pad_h(-jnp.exp(ssm_a_log[j].astype(F32)))
            dfull = row(jnp.repeat(ssm_d[j].astype(F32), ssm_p))
            nw_ssm = row(ssm_norm_w[j])
            w_out = ssm_w_out[j].astype(BF16)

            zx_p = _rms_linear(xp, nw_mix, w_main, BF16, name="ssm_in_prompt")
            dt_p = _rms_linear(xp, nw_mix, w_dt, F32, name="ssm_dt_prompt")
            yg_p, st_p = _ssd_prompt(zx_p, dt_p, cwx, cbx, cwb, cbb, dtb, a_neg, dfull, nw_ssm, tt,
                                     ex2, bp, seq, chunk, groups, d_state)
            xp = _linear_res(yg_p, w_out, xp, name="ssm_out_prompt")
            outs["ssm_sp"].append(jnp.swapaxes(st_p, 2, 3).reshape(bp, ssm_heads, ssm_p, d_state))
            outs["ssm_cp"].append(
                zx_p.reshape(bp, seq, -1)[:, seq - (ssm_k - 1):, d_inner:].astype(F32))

            zx_s = _rms_linear(xs, nw_mix, w_main, BF16, tm=1024, name="ssm_in_sample")
            dt_s = _rms_linear(xs, nw_mix, w_dt, F32, tm=1024, name="ssm_dt_sample")
            hist = _pad_slots(state_ssm_conv[j], tok0 - (ssm_k - 1))
            yg_s, st_s = _ssd_sample(zx_s, hist, dt_s, cwx, cbx, cwb, cbb, dtb, a_neg, dfull, nw_ssm,
                                     ex2, sel, state_ssm[j].reshape(bs, d_inner // LANES, LANES, d_state),
                                     groups, d_state, tok0)
            xs = _linear_res(yg_s, w_out, xs, name="ssm_out_sample")
            outs["ssm_ss"].append(st_s.reshape(bs, ssm_heads, ssm_p, d_state))
            ext = jnp.concatenate(
                [state_ssm_conv[j], zx_s.reshape(bs, SLOTS, -1)[:, tok0:, d_inner:].astype(F32)], axis=1)
            outs["ssm_cs"].append(ext[:, -(ssm_k - 1):])
        else:
            wq, wk, wv = (sb_wqkv[j][:, t * sb_w:(t + 1) * sb_w] for t in range(3))
            w_p = jnp.concatenate([_widen_heads(wq * scale, [h % 2 for h in range(sb_heads)]), wk, wv],
                                  axis=1).astype(BF16)
            w_s = jnp.concatenate([wq * scale, wk, wv], axis=1).astype(BF16)
            bias = sb_logit_bias[j].astype(F32)
            wo = sb_wo[j].astype(BF16)
            kcol = sb_heads * LANES

            qkv_p = _rms_linear(xp, nw_mix, w_p, BF16, name="sb_qkv_prompt")
            o_p = _sb_prompt(qkv_p, bias, uu_blk, bp, seq, sb_heads)
            xp = _linear_res(o_p, wo, xp, name="sb_out_prompt")
            kv_p = qkv_p[:, kcol:].astype(F32)
            outs["sb_kp"].append(kv_p[:, :sb_w].reshape(bp, seq, sb_heads, HEAD_DIM))
            outs["sb_vp"].append(kv_p[:, sb_w:].reshape(bp, seq, sb_heads, HEAD_DIM))

            qkv_s = _rms_linear(xs, nw_mix, w_s, F32, tm=1024, name="sb_qkv_sample")
            n_pool = cache_sb_k.shape[1]
            o_s = _sb_sample(qkv_s, cache_sb_k[j].reshape(n_pool, page_size, sb_w),
                             cache_sb_v[j].reshape(n_pool, page_size, sb_w), page_table,
                             jnp.tile(bias, n_tok).reshape(-1, 1), sb_head_mask, uu_page,
                             sb_heads, n_tok, tok0)
            xs = _linear_res(o_s, wo, xs, name="sb_out_sample")
            kv_s = qkv_s.reshape(bs, SLOTS, -1)[:, tok0:, sb_w:]
            outs["sb_ks"].append(kv_s[..., :sb_w].reshape(bs, n_tok, sb_heads, HEAD_DIM))
            outs["sb_vs"].append(kv_s[..., sb_w:].reshape(bs, n_tok, sb_heads, HEAD_DIM))

        nw_x = row(norm_xattn_w[i])
        wq_m = _widen_heads(mem_wq[i] * scale, [h % 2 for h in range(mem_heads)]).astype(BF16)
        wo_m = mem_wo[i].astype(BF16)
        k_m = mem_kv_b[:, (2 * i) * mem_w:(2 * i + 1) * mem_w]
        v_m = mem_kv_b[:, (2 * i + 1) * mem_w:(2 * i + 2) * mem_w]
        xp = _mem_attn_prompt(xp, nw_x, wq_m, k_m, v_m, wo_m, seq, n_mem)
        xs = _mem_attn_sample(xs, nw_x, wq_m, cache_mem_k[i].reshape(bs, n_mem, mem_w),
                              cache_mem_v[i].reshape(bs, n_mem, mem_w), wo_m)

        nw_f = row(norm_ffn_w[i])
        w_in = ffn_w_in[i].astype(BF16)
        w_out = ffn_w_out[i].astype(BF16)
        cw, cb = ffn_conv_w[i].astype(F32), row(ffn_conv_b[i])
        gu_p = _rms_linear(xp, nw_f, w_in, BF16, name="ffn_in_prompt")
        xp = _glu_prompt(gu_p, cw, cb, w_out, xp, seq)
        outs["ffn_cp"].append(gu_p.reshape(bp, seq, -1)[:, seq - (ffn_k - 1):, :d_ff].astype(F32))
        gu_s = _rms_linear(xs, nw_f, w_in, BF16, tm=1024, name="ffn_in_sample")
        xs = _glu_sample(gu_s, ffn_hist[i], cw, cb, w_out, xs, tok0)
        ext = jnp.concatenate(
            [state_ffn_conv[i], gu_s.reshape(bs, SLOTS, -1)[:, tok0:, :d_ff].astype(F32)], axis=1)
        outs["ffn_cs"].append(ext[:, -(ffn_k - 1):])

    y_prompt = _rmsnorm(xp, row(final_norm_w)).reshape(bp, seq, d)
    y_sample = _rmsnorm(xs, row(final_norm_w)).reshape(bs, SLOTS, d)[:, tok0:]
    st = lambda k: jnp.stack(outs[k])
    return (y_prompt, y_sample, st("swa_kp"), st("swa_vp"), st("swa_ks"), st("swa_vs"),
            st("ssm_sp"), st("ssm_cp"), st("ssm_ss"), st("ssm_cs"),
            st("sb_kp"), st("sb_vp"), st("sb_ks"), st("sb_vs"),
            mem_k_prompt, mem_v_prompt, st("ffn_cp"), st("ffn_cs"))
```

```python
import functools
import math

import jax
import jax.numpy as jnp
from jax import lax
from jax.experimental import pallas as pl
from jax.experimental.pallas import tpu as pltpu

F32 = jnp.float32
BF16 = jnp.bfloat16
EPS = 1e-6
HEAD_DIM = 64
LANES = 128
SUBLANES = 8
SLOTS = SUBLANES
NEG = -1e30
VMEM_LIMIT = 56 * 1024 * 1024
LOG2E = 1.0 / math.log(2.0)
NT = (((1,), (1,)), ((), ()))
TN = (((0,), (0,)), ((), ()))


def _params(*sem):
    return pltpu.CompilerParams(dimension_semantics=sem, vmem_limit_bytes=VMEM_LIMIT)


def _tile(n, want):
    if n <= want:
        return n
    t = (want // LANES) * LANES
    while t > LANES and n % t:
        t -= LANES
    assert n % t == 0, (n, want)
    return t


def _dot(a, b):
    return jnp.dot(a, b, preferred_element_type=F32)


def _dotg(a, b, dims):
    return lax.dot_general(a, b, dims, preferred_element_type=F32)


def _hi_lo(v, axis):
    hi = v.astype(BF16).astype(F32)
    return jnp.concatenate([hi, v - hi], axis=axis).astype(BF16)


def _softplus(z):
    return jnp.maximum(z, 0.0) + jnp.log1p(jnp.exp(-jnp.abs(z)))


def _silu(v):
    return v * jax.nn.sigmoid(v)


def _rms(x, w):
    ms = jnp.mean(x * x, axis=-1, keepdims=True)
    return x * lax.rsqrt(ms + EPS) * w


def _rms_linear_kernel(x_ref, nw_ref, w_ref, o_ref, xn_ref, *, tn):
    xn_ref[...] = _rms(x_ref[...], nw_ref[...]).astype(BF16)
    for c0 in range(0, w_ref.shape[1], tn):
        o_ref[:, c0:c0 + tn] = _dot(xn_ref[...], w_ref[:, c0:c0 + tn]).astype(o_ref.dtype)


def _rms_linear(x, nw, w, out_dtype, *, tm=512, tn=512, name="rms_linear"):
    n, d = x.shape
    f = w.shape[1]
    tm, tn = _tile(n, tm), _tile(f, tn)
    return pl.pallas_call(
        functools.partial(_rms_linear_kernel, tn=tn),
        out_shape=jax.ShapeDtypeStruct((n, f), out_dtype),
        grid=(n // tm,),
        in_specs=[pl.BlockSpec((tm, d), lambda i: (i, 0)),
                  pl.BlockSpec((1, d), lambda i: (0, 0)),
                  pl.BlockSpec((d, f), lambda i: (0, 0), pipeline_mode=pl.Buffered(1))],
        out_specs=pl.BlockSpec((tm, f), lambda i: (i, 0)),
        scratch_shapes=[pltpu.VMEM((tm, d), BF16)],
        compiler_params=_params("parallel"),
        name=name,
    )(x, nw, w)


def _linear_res_kernel(y_ref, w_ref, r_ref, o_ref):
    o_ref[...] = r_ref[...] + _dot(y_ref[...].astype(BF16), w_ref[...])


def _linear_res(y, w, res, *, tm=512, name="linear_res"):
    n, k = y.shape
    d = w.shape[1]
    tm = _tile(n, tm)
    return pl.pallas_call(
        _linear_res_kernel,
        out_shape=jax.ShapeDtypeStruct((n, d), F32),
        grid=(n // tm,),
        in_specs=[pl.BlockSpec((tm, k), lambda i: (i, 0)),
                  pl.BlockSpec((k, d), lambda i: (0, 0)),
                  pl.BlockSpec((tm, d), lambda i: (i, 0))],
        out_specs=pl.BlockSpec((tm, d), lambda i: (i, 0)),
        compiler_params=_params("parallel"),
        name=name,
    )(y, w, res)


def _rmsnorm_kernel(x_ref, nw_ref, o_ref):
    o_ref[...] = _rms(x_ref[...], nw_ref[...])


def _rmsnorm(x, nw, *, tm=512):
    n, d = x.shape
    tm = _tile(n, tm)
    return pl.pallas_call(
        _rmsnorm_kernel,
        out_shape=jax.ShapeDtypeStruct((n, d), F32),
        grid=(n // tm,),
        in_specs=[pl.BlockSpec((tm, d), lambda i: (i, 0)),
                  pl.BlockSpec((1, d), lambda i: (0, 0))],
        out_specs=pl.BlockSpec((tm, d), lambda i: (i, 0)),
        compiler_params=_params("parallel"),
        name="final_rmsnorm",
    )(x, nw)


def _shift_rows(cur, halo, d, row):
    if d == 0:
        return cur
    out = pltpu.roll(cur, d, 0)
    for r in range(d):
        src = SUBLANES - d + r
        out = jnp.where(row == r, halo[src:src + 1], out)
    return out


def _conv_taps(cur, halo, w_ref, b_ref, sl, row):
    k = w_ref.shape[0]
    acc = b_ref[:, sl] + w_ref[k - 1:k, sl] * cur
    for d in range(1, k):
        acc = acc + w_ref[k - 1 - d:k - d, sl] * _shift_rows(cur, halo, d, row)
    return acc


def _conv_taps_slots(ext, w_ref, b_ref, sl):
    k = w_ref.shape[0]
    acc = b_ref[:, sl] + w_ref[k - 1:k, sl] * ext
    for d in range(1, k):
        acc = acc + w_ref[k - 1 - d:k - d, sl] * pltpu.roll(ext, d, 0)
    return acc


def _glu_prompt_kernel(g_ref, u_ref, halo_ref, cw_ref, cb_ref, w_ref, r_ref, o_ref, h_ref,
                       *, tiles_per_seq, lane_chunk):
    tm, f = g_ref.shape
    not_first = (pl.program_id(0) % tiles_per_seq) != 0
    row = lax.broadcasted_iota(jnp.int32, (tm, lane_chunk), 0)
    for c0 in range(0, f, lane_chunk):
        sl = slice(c0, c0 + lane_chunk)
        g = g_ref[:, sl].astype(F32)
        halo = jnp.where(not_first, halo_ref[:, sl].astype(F32), 0.0)
        conv = _conv_taps(g, halo, cw_ref, cb_ref, sl, row)
        h_ref[:, sl] = (_silu(conv) * u_ref[:, sl].astype(F32)).astype(BF16)
    o_ref[...] = r_ref[...] + _dot(h_ref[...], w_ref[...])


def _glu_prompt(gu, cw, cb, w_out, res, seq_len, *, tm=512):
    n, f2 = gu.shape
    f = f2 // 2
    d = w_out.shape[1]
    tm = _tile(seq_len, tm)
    hb = tm // SUBLANES
    lane_chunk = 256 if f % 256 == 0 else LANES
    kern = functools.partial(_glu_prompt_kernel, tiles_per_seq=seq_len // tm, lane_chunk=lane_chunk)
    return pl.pallas_call(
        kern,
        out_shape=jax.ShapeDtypeStruct((n, d), F32),
        grid=(n // tm,),
        in_specs=[pl.BlockSpec((tm, f), lambda i: (i, 0)),
                  pl.BlockSpec((tm, f), lambda i: (i, 1)),
                  pl.BlockSpec((SUBLANES, f), lambda i: (jnp.maximum(i * hb - 1, 0), 0)),
                  pl.BlockSpec(cw.shape, lambda i: (0, 0)),
                  pl.BlockSpec((1, f), lambda i: (0, 0)),
                  pl.BlockSpec((f, d), lambda i: (0, 0)),
                  pl.BlockSpec((tm, d), lambda i: (i, 0))],
        out_specs=pl.BlockSpec((tm, d), lambda i: (i, 0)),
        scratch_shapes=[pltpu.VMEM((tm, f), BF16)],
        compiler_params=_params("parallel"),
        name="glu_prompt",
    )(gu, gu, gu, cw, cb, w_out, res)


def _glu_sample_kernel(g_ref, u_ref, hist_ref, cw_ref, cb_ref, w_ref, r_ref, o_ref, h_ref,
                       *, tok0, lane_chunk):
    tm, f = g_ref.shape
    row = lax.broadcasted_iota(jnp.int32, (tm, lane_chunk), 0)
    is_hist = (row % SLOTS) < tok0
    for c0 in range(0, f, lane_chunk):
        sl = slice(c0, c0 + lane_chunk)
        ext = jnp.where(is_hist, hist_ref[:, sl], g_ref[:, sl].astype(F32))
        conv = _conv_taps_slots(ext, cw_ref, cb_ref, sl)
        h_ref[:, sl] = (_silu(conv) * u_ref[:, sl].astype(F32)).astype(BF16)
    o_ref[...] = r_ref[...] + _dot(h_ref[...], w_ref[...])


def _glu_sample(gu, hist, cw, cb, w_out, res, tok0, *, tm=256):
    n, f2 = gu.shape
    f = f2 // 2
    d = w_out.shape[1]
    tm = _tile(n, tm)
    lane_chunk = 256 if f % 256 == 0 else LANES
    kern = functools.partial(_glu_sample_kernel, tok0=tok0, lane_chunk=lane_chunk)
    return pl.pallas_call(
        kern,
        out_shape=jax.ShapeDtypeStruct((n, d), F32),
        grid=(n // tm,),
        in_specs=[pl.BlockSpec((tm, f), lambda i: (i, 0)),
                  pl.BlockSpec((tm, f), lambda i: (i, 1)),
                  pl.BlockSpec((tm, f), lambda i: (i, 0)),
                  pl.BlockSpec(cw.shape, lambda i: (0, 0)),
                  pl.BlockSpec((1, f), lambda i: (0, 0)),
                  pl.BlockSpec((f, d), lambda i: (0, 0)),
                  pl.BlockSpec((tm, d), lambda i: (i, 0))],
        out_specs=pl.BlockSpec((tm, d), lambda i: (i, 0)),
        scratch_shapes=[pltpu.VMEM((tm, f), BF16)],
        compiler_params=_params("parallel"),
        name="glu_sample",
    )(gu, gu, hist, cw, cb, w_out, res)


def _softmax_pv(s, v, transposed):
    m = jnp.max(s, axis=-1, keepdims=True)
    p = jnp.exp(s - m)
    l = jnp.sum(p, axis=-1, keepdims=True)
    pv = _dotg(p.astype(BF16), v, NT) if transposed else _dot(p.astype(BF16), v)
    return pv / l


def _mem_heads(q, k, v, n_heads, transposed=False):
    half0 = lax.broadcasted_iota(jnp.int32, (q.shape[0], LANES), 1) < HEAD_DIM
    slabs = []
    for s in range(n_heads // 2):
        sl = slice(s * LANES, (s + 1) * LANES)
        ks, vs = (k[sl, :], v[sl, :]) if transposed else (k[:, sl], v[:, sl])
        o = []
        for c in range(2):
            qc = q[:, (2 * s + c) * LANES:(2 * s + c + 1) * LANES]
            scores = _dot(qc, ks) if transposed else _dotg(qc, ks, NT)
            o.append(_softmax_pv(scores, vs, transposed))
        slabs.append(jnp.where(half0, o[0], o[1]))
    return jnp.concatenate(slabs, axis=1)


def _mem_attn_prompt_kernel(x_ref, nw_ref, wq_ref, k_ref, v_ref, wo_ref, o_ref, *, n_heads):
    x = x_ref[...]
    q = _dot(_rms(x, nw_ref[...]).astype(BF16), wq_ref[...]).astype(BF16)
    o = _mem_heads(q, k_ref[...], v_ref[...], n_heads)
    o_ref[...] = x + _dot(o.astype(BF16), wo_ref[...])


def _mem_attn_prompt(x, nw, wq, k, v, wo, seq_len, n_mem, *, tm=512):
    n, d = x.shape
    hw = k.shape[1]
    tm = _tile(seq_len, tm)
    tps = seq_len // tm
    kern = functools.partial(_mem_attn_prompt_kernel, n_heads=hw // HEAD_DIM)
    return pl.pallas_call(
        kern,
        out_shape=jax.ShapeDtypeStruct((n, d), F32),
        grid=(n // tm,),
        in_specs=[pl.BlockSpec((tm, d), lambda i: (i, 0)),
                  pl.BlockSpec((1, d), lambda i: (0, 0)),
                  pl.BlockSpec(wq.shape, lambda i: (0, 0)),
                  pl.BlockSpec((n_mem, hw), lambda i: (i // tps, 0)),
                  pl.BlockSpec((n_mem, hw), lambda i: (i // tps, 0)),
                  pl.BlockSpec(wo.shape, lambda i: (0, 0))],
        out_specs=pl.BlockSpec((tm, d), lambda i: (i, 0)),
        compiler_params=_params("parallel"),
        name="mem_attn_prompt",
    )(x, nw, wq, k, v, wo)


def _mem_attn_sample_kernel(x_ref, nw_ref, wq_ref, k_ref, v_ref, wo_ref, o_ref, *, n_heads, bb):
    x = x_ref[...]
    q = _dot(_rms(x, nw_ref[...]).astype(BF16), wq_ref[...])
    outs = []
    for b in range(bb):
        qb = q[b * SLOTS:(b + 1) * SLOTS].astype(BF16)
        outs.append(_mem_heads(qb, k_ref[b].astype(BF16), v_ref[b].astype(BF16), n_heads,
                               transposed=True))
    o = jnp.concatenate(outs, axis=0)
    o_ref[...] = x + _dot(o.astype(BF16), wo_ref[...])


def _mem_attn_sample(x, nw, wq, k, v, wo, layer, *, bb=8):
    n, d = x.shape
    bs = n // SLOTS
    _, hw, n_mem = k.shape
    bb = min(bb, bs)
    base = layer * (bs // bb)
    kern = functools.partial(_mem_attn_sample_kernel, n_heads=hw // HEAD_DIM, bb=bb)
    return pl.pallas_call(
        kern,
        out_shape=jax.ShapeDtypeStruct((n, d), F32),
        grid=(bs // bb,),
        in_specs=[pl.BlockSpec((bb * SLOTS, d), lambda i: (i, 0)),
                  pl.BlockSpec((1, d), lambda i: (0, 0)),
                  pl.BlockSpec(wq.shape, lambda i: (0, 0)),
                  pl.BlockSpec((bb, hw, n_mem), lambda i: (base + i, 0, 0)),
                  pl.BlockSpec((bb, hw, n_mem), lambda i: (base + i, 0, 0)),
                  pl.BlockSpec(wo.shape, lambda i: (0, 0))],
        out_specs=pl.BlockSpec((bb * SLOTS, d), lambda i: (i, 0)),
        compiler_params=_params("parallel"),
        name="mem_attn_sample",
    )(x, nw, wq, k, v, wo)


def _sink_softmax_pv(s, sink, v):
    m = jnp.maximum(jnp.max(s, axis=-1, keepdims=True), sink)
    p = jnp.exp(s - m)
    l = jnp.sum(p, axis=-1, keepdims=True) + jnp.exp(sink - m)
    return _dot(p.astype(BF16), v) / l


def _swa_prompt_kernel(sm_ref, q_ref, kc_ref, vc_ref, kp_ref, vp_ref, o_ref, *, n_heads, group):
    w = q_ref.shape[0]
    n = pl.program_id(1)
    a = lax.broadcasted_iota(jnp.int32, (w, 2 * w), 0)
    j = lax.broadcasted_iota(jnp.int32, (w, 2 * w), 1)
    dist = (a + w - j).astype(F32)
    has_prev = jnp.where(n > 0, 0, w)
    bias_mask = jnp.where((j >= a) & (j <= a + w) & (j >= has_prev), 0.0, NEG)
    half0 = lax.broadcasted_iota(jnp.int32, (w, LANES), 1) < HEAD_DIM
    kv_pairs = n_heads // group // 2
    for p in range(kv_pairs):
        sl = slice(p * LANES, (p + 1) * LANES)
        kpair = jnp.concatenate([kp_ref[:, sl], kc_ref[:, sl]], axis=0)
        vpair = jnp.concatenate([vp_ref[:, sl], vc_ref[:, sl]], axis=0)
        for r in range(group):
            o = []
            for c in range(2):
                h = (2 * p + c) * group + r
                s = _dotg(q_ref[:, h * LANES:(h + 1) * LANES], kpair, NT)
                s = s - sm_ref[0, h] * dist + bias_mask
                o.append(_sink_softmax_pv(s, sm_ref[1, h], vpair))
            slab = p * group + r
            o_ref[:, slab * LANES:(slab + 1) * LANES] = jnp.where(half0, o[0], o[1]).astype(BF16)


def _swa_prompt(qkv, sm, batch, seq_len, window, n_heads, group):
    n = qkv.shape[0]
    nb = seq_len // window
    qw = n_heads * LANES
    kvw = n_heads // group * HEAD_DIM
    kcol, vcol = qw // kvw, qw // kvw + 1
    kern = functools.partial(_swa_prompt_kernel, n_heads=n_heads, group=group)
    cur = lambda b, i: b * nb + i
    prev = lambda b, i: b * nb + jnp.maximum(i - 1, 0)
    return pl.pallas_call(
        kern,
        out_shape=jax.ShapeDtypeStruct((n, n_heads * HEAD_DIM), BF16),
        grid=(batch, nb),
        in_specs=[pl.BlockSpec(memory_space=pltpu.SMEM),
                  pl.BlockSpec((window, qw), lambda b, i: (cur(b, i), 0)),
                  pl.BlockSpec((window, kvw), lambda b, i: (cur(b, i), kcol)),
                  pl.BlockSpec((window, kvw), lambda b, i: (cur(b, i), vcol)),
                  pl.BlockSpec((window, kvw), lambda b, i: (prev(b, i), kcol)),
                  pl.BlockSpec((window, kvw), lambda b, i: (prev(b, i), vcol))],
        out_specs=pl.BlockSpec((window, n_heads * HEAD_DIM), lambda b, i: (cur(b, i), 0)),
        compiler_params=_params("parallel", "parallel"),
        name="swa_prompt",
    )(sm, qkv, qkv, qkv, qkv, qkv)


def _swa_sample_kernel(sm_ref, q_ref, kn_ref, vn_ref, kc_ref, vc_ref, o_ref,
                       *, n_heads, group, bb, tok0):
    w = kc_ref.shape[1]
    rows = group * SLOTS
    row = lax.broadcasted_iota(jnp.int32, (rows, 2 * w), 0)
    j = lax.broadcasted_iota(jnp.int32, (rows, 2 * w), 1)
    tq = row % SLOTS - tok0
    in_cache = j < w
    dist_i = jnp.where(in_cache, w + tq - j, tq + tok0 + w - j)
    mask_cache = jnp.where(j >= tq, 0.0, NEG)
    mask_new = jnp.where((j >= w + tok0) & (j < w + SLOTS) & (dist_i >= 0), 0.0, NEG)
    bias_mask = jnp.where(in_cache, mask_cache, mask_new)
    dist = dist_i.astype(F32)
    rgrp = lax.broadcasted_iota(jnp.int32, (rows, 1), 0) // SLOTS
    half0 = lax.broadcasted_iota(jnp.int32, (SLOTS, LANES), 1) < HEAD_DIM
    zeros = jnp.zeros((w - SLOTS, LANES), F32)
    q_all = q_ref[...].astype(F32)
    kn_all = kn_ref[...].astype(F32)
    vn_all = vn_ref[...].astype(F32)
    kv_pairs = n_heads // group // 2
    out_rows = []
    for b in range(bb):
        rs = slice(b * SLOTS, (b + 1) * SLOTS)
        slabs = [None] * (n_heads // 2)
        for p in range(kv_pairs):
            sl = slice(p * LANES, (p + 1) * LANES)
            kfull = jnp.concatenate([kc_ref[b, :, sl], kn_all[rs, sl], zeros], axis=0).astype(BF16)
            vfull = jnp.concatenate([vc_ref[b, :, sl], vn_all[rs, sl], zeros], axis=0).astype(BF16)
            o = []
            for c in range(2):
                heads = [(2 * p + c) * group + r for r in range(group)]
                qs = jnp.concatenate([q_all[rs, h * LANES:(h + 1) * LANES] for h in heads], axis=0)
                slope = jnp.zeros((rows, 1), F32)
                sink = jnp.zeros((rows, 1), F32)
                for r, h in enumerate(heads):
                    slope = jnp.where(rgrp == r, sm_ref[0, h], slope)
                    sink = jnp.where(rgrp == r, sm_ref[1, h], sink)
                s = _dotg(qs.astype(BF16), kfull, NT) - slope * dist + bias_mask
                o.append(_sink_softmax_pv(s, sink, vfull))
            for r in range(group):
                piece = slice(r * SLOTS, (r + 1) * SLOTS)
                slabs[p * group + r] = jnp.where(half0, o[0][piece], o[1][piece])
        out_rows.append(jnp.concatenate(slabs, axis=1))
    o_ref[...] = jnp.concatenate(out_rows, axis=0).astype(BF16)


def _swa_sample(qkv, sm, kc, vc, n_heads, group, tok0, *, bb=8):
    n = qkv.shape[0]
    bs, window, kvw = kc.shape
    bb = min(bb, bs)
    qw = n_heads * LANES
    kcol, vcol = qw // kvw, qw // kvw + 1
    kern = functools.partial(_swa_sample_kernel, n_heads=n_heads, group=group, bb=bb, tok0=tok0)
    return pl.pallas_call(
        kern,
        out_shape=jax.ShapeDtypeStruct((n, n_heads * HEAD_DIM), BF16),
        grid=(bs // bb,),
        in_specs=[pl.BlockSpec(memory_space=pltpu.SMEM),
                  pl.BlockSpec((bb * SLOTS, qw), lambda i: (i, 0)),
                  pl.BlockSpec((bb * SLOTS, kvw), lambda i: (i, kcol)),
                  pl.BlockSpec((bb * SLOTS, kvw), lambda i: (i, vcol)),
                  pl.BlockSpec((bb, window, kvw), lambda i: (i, 0, 0)),
                  pl.BlockSpec((bb, window, kvw), lambda i: (i, 0, 0))],
        out_specs=pl.BlockSpec((bb * SLOTS, n_heads * HEAD_DIM), lambda i: (i, 0)),
        compiler_params=_params("parallel"),
        name="swa_sample",
    )(sm, qkv, qkv, qkv, kc, vc)


def _sb_prompt_kernel(bias_ref, q_ref, k_ref, v_ref, uu_ref, o_ref, qs_ref, brow_ref, acc_ref,
                      r_ref, *, tk):
    tq = q_ref.shape[0]
    m = 2 * tq
    pair = pl.program_id(1)
    i = pl.program_id(2)
    for c in range(2):
        qs_ref[c * tq:(c + 1) * tq, :] = q_ref[:, c * LANES:(c + 1) * LANES]
        brow_ref[c * tq:(c + 1) * tq, :] = jnp.full((tq, tk), bias_ref[2 * pair + c], F32)
    acc_ref[...] = jnp.zeros_like(acc_ref)
    r_ref[...] = jnp.zeros_like(r_ref)
    qpos = lax.broadcasted_iota(jnp.int32, (m, tk), 0) % tq + i * tq
    kcol = lax.broadcasted_iota(jnp.int32, (m, tk), 1)

    def visit(j, masked):
        start = pl.multiple_of(j * tk, tk)
        kj = k_ref[pl.ds(start, tk), :]
        vj = v_ref[pl.ds(start, tk), :]
        z = _dotg(qs_ref[...], kj, NT) + brow_ref[...]
        sp = jnp.maximum(z, 0.0) + jnp.log(1.0 + jnp.exp2(-jnp.abs(z))) * LOG2E
        if masked:
            valid = kcol + j * tk < qpos
            sp = jnp.where(valid, sp, 0.0)
        incl = _dot(sp.astype(BF16), uu_ref[...])
        r = r_ref[...]
        e = z + incl + jnp.concatenate([r] * (tk // LANES), axis=1)
        if masked:
            e = jnp.where(valid, e, NEG)
        acc_ref[...] = acc_ref[...] + _dot(jnp.exp2(e).astype(BF16), vj)
        r_ref[...] = r + incl[:, 0:1]

    per_q = tq // tk
    for u in range(per_q):
        visit((i + 1) * per_q - 1 - u, True)

    def body(jj, carry):
        visit(i * per_q - 1 - jj, False)
        return carry

    lax.fori_loop(0, i * per_q, body, 0)
    half0 = lax.broadcasted_iota(jnp.int32, (tq, LANES), 1) < HEAD_DIM
    o_ref[...] = jnp.where(half0, acc_ref[0:tq, :], acc_ref[tq:m, :]).astype(BF16)


def _sb_prompt(qkv, bias2, uu, batch, seq_len, n_heads, tq, tk):
    n = qkv.shape[0]
    nq = seq_len // tq
    pairs = n_heads // 2
    kcol0 = n_heads
    vcol0 = n_heads + pairs
    return pl.pallas_call(
        functools.partial(_sb_prompt_kernel, tk=tk),
        out_shape=jax.ShapeDtypeStruct((n, n_heads * HEAD_DIM), BF16),
        grid=(batch, pairs, nq),
        in_specs=[pl.BlockSpec(memory_space=pltpu.SMEM),
                  pl.BlockSpec((tq, 2 * LANES), lambda b, p, i: (b * nq + i, p)),
                  pl.BlockSpec((seq_len, LANES), lambda b, p, i: (b, kcol0 + p)),
                  pl.BlockSpec((seq_len, LANES), lambda b, p, i: (b, vcol0 + p)),
                  pl.BlockSpec(uu.shape, lambda b, p, i: (0, 0))],
        out_specs=pl.BlockSpec((tq, LANES), lambda b, p, i: (b * nq + i, p)),
        scratch_shapes=[pltpu.VMEM((2 * tq, LANES), BF16), pltpu.VMEM((2 * tq, tk), F32),
                        pltpu.VMEM((2 * tq, LANES), F32), pltpu.VMEM((2 * tq, LANES), F32)],
        compiler_params=_params("parallel", "parallel", "arbitrary"),
        name="sb_prompt",
    )(bias2, qkv, qkv, qkv, uu)


def _sb_weights(z, uu, r_prev, valid):
    kb = z.shape[1]
    lk = -_softplus(z)
    if valid is not None:
        lk = jnp.where(valid, lk, 0.0)
    both = _dot(_hi_lo(lk, 1), uu)
    e = z + both[:, :kb] + r_prev
    if valid is not None:
        e = jnp.where(valid, e, NEG)
    return jnp.exp(e).astype(BF16), both[:, kb:]


def _sb_sample_kernel(pt_ref, q_ref, kn_ref, vn_ref, bias_ref, mask_ref, uu_ref, *rest,
                      n_heads, n_tok, tok0, pages_per_step):
    k_refs = rest[:pages_per_step]
    v_refs = rest[pages_per_step:2 * pages_per_step]
    o_ref, qb_ref, acc_ref, r_ref = rest[2 * pages_per_step:]
    hw, ps = k_refs[0].shape
    step = pl.program_id(1)
    rows = n_tok * n_heads

    @pl.when(step == 0)
    def _():
        q = q_ref[...]
        for t in range(n_tok):
            qb_ref[t * n_heads:(t + 1) * n_heads, :] = (
                q[tok0 + t:tok0 + t + 1, :] * mask_ref[...]).astype(BF16)
        zeros = jnp.zeros((ps - SLOTS, hw), F32)
        kn = jnp.concatenate([kn_ref[...], zeros], axis=0).astype(BF16)
        vn = jnp.concatenate([vn_ref[...], zeros], axis=0).astype(BF16)
        tq = lax.broadcasted_iota(jnp.int32, (rows, ps), 0) // n_heads
        j = lax.broadcasted_iota(jnp.int32, (rows, ps), 1)
        a, tot = _sb_weights(_dotg(qb_ref[...], kn, NT) + bias_ref[...], uu_ref[...],
                             jnp.zeros((rows, ps), F32), (j >= tok0) & (j < tok0 + tq))
        acc_ref[...] = _dot(a, vn)
        r_ref[...] = tot

    zs = [_dot(qb_ref[...], k_refs[g][...].astype(BF16)) + bias_ref[...]
          for g in range(pages_per_step)]
    boths = [_dot(_hi_lo(-_softplus(z), 1), uu_ref[...]) for z in zs]
    r = r_ref[...]
    acc = acc_ref[...]
    for g in range(pages_per_step):
        a = jnp.exp(zs[g] + boths[g][:, :ps] + r).astype(BF16)
        acc = acc + _dotg(a, v_refs[g][...].astype(BF16), NT)
        r = r + boths[g][:, ps:]
    acc_ref[...] = acc
    r_ref[...] = r

    @pl.when(step == pl.num_programs(1) - 1)
    def _():
        o_ref[0:tok0, :] = jnp.zeros((tok0, hw), o_ref.dtype)
        for t in range(n_tok):
            blk = acc_ref[t * n_heads:(t + 1) * n_heads, :] * mask_ref[...]
            o_ref[tok0 + t:tok0 + t + 1, :] = jnp.sum(blk, axis=0, keepdims=True)


def _sb_sample(qkv, pool_k, pool_v, pool_base, page_table, bias_col, head_mask, uu, n_heads,
               n_tok, tok0, *, pages_per_step=8):
    n = qkv.shape[0]
    bs, n_pages = page_table.shape
    _, hw, ps = pool_k.shape
    g = pages_per_step
    while n_pages % g:
        g //= 2
    steps = n_pages // g
    rows = n_tok * n_heads
    pt = page_table.reshape(-1)

    def page_map(gi):
        return lambda b, s, pt_ref: (
            pool_base + pt_ref[b * n_pages + n_pages - 1 - (s * g + gi)], 0, 0)

    page_specs = [pl.BlockSpec((None, hw, ps), page_map(gi)) for gi in range(g)]
    kern = functools.partial(_sb_sample_kernel, n_heads=n_heads, n_tok=n_tok, tok0=tok0,
                             pages_per_step=g)
    grid_spec = pltpu.PrefetchScalarGridSpec(
        num_scalar_prefetch=1,
        grid=(bs, steps),
        in_specs=[pl.BlockSpec((SLOTS, hw), lambda b, s, pt_ref: (b, 0)),
                  pl.BlockSpec((SLOTS, hw), lambda b, s, pt_ref: (b, 1)),
                  pl.BlockSpec((SLOTS, hw), lambda b, s, pt_ref: (b, 2)),
                  pl.BlockSpec((rows, 1), lambda b, s, pt_ref: (0, 0)),
                  pl.BlockSpec((n_heads, hw), lambda b, s, pt_ref: (0, 0)),
                  pl.BlockSpec(uu.shape, lambda b, s, pt_ref: (0, 0))] + page_specs + page_specs,
        out_specs=pl.BlockSpec((SLOTS, hw), lambda b, s, pt_ref: (b, 0)),
        scratch_shapes=[pltpu.VMEM((rows, hw), BF16), pltpu.VMEM((rows, hw), F32),
                        pltpu.VMEM((rows, LANES), F32)],
    )
    return pl.pallas_call(
        kern,
        out_shape=jax.ShapeDtypeStruct((n, hw), F32),
        grid_spec=grid_spec,
        compiler_params=_params("parallel", "arbitrary"),
        name="sb_sample",
    )(pt, qkv, qkv, qkv, bias_col, head_mask, uu, *([pool_k] * g), *([pool_v] * g))


def _expand_heads(v, ex2):
    return _dot(_hi_lo(v, 1), ex2)


def _ssd_diag(acs, cb, xs, same_seq, half_masks, h0):
    acs_t = acs.T
    y = None
    for c in range(2):
        h = h0 + c
        seg = acs[:, h:h + 1] - acs_t[h:h + 1, :]
        m = (cb * jnp.exp(jnp.where(same_seq, seg, NEG))).astype(BF16)
        term = _dot(m, jnp.where(half_masks[c], xs, 0.0).astype(BF16))
        y = term if y is None else y + term
    return y


def _ssd_prompt_kernel(z_ref, x_ref, bc_ref, xh_ref, bch_ref, dt_ref, cwx_ref, cbx_ref, cwb_ref,
                       cbb_ref, dtb_ref, a_ref, d_ref, nw_ref, tt_ref, ex_ref, yg_ref, st_ref,
                       state_ref, *, groups, d_state, lane_chunk):
    lc, di = x_ref.shape
    c_idx = pl.program_id(1)
    not_first = c_idx > 0

    @pl.when(c_idx == 0)
    def _():
        state_ref[...] = jnp.zeros_like(state_ref)

    row = lax.broadcasted_iota(jnp.int32, (lc, lane_chunk), 0)

    def conv_silu(cur_ref, halo_ref, w_ref, b_ref):
        outs = []
        for c0 in range(0, cur_ref.shape[1], lane_chunk):
            sl = slice(c0, c0 + lane_chunk)
            halo = jnp.where(not_first, halo_ref[:, sl].astype(F32), 0.0)
            outs.append(_silu(_conv_taps(cur_ref[:, sl].astype(F32), halo, w_ref, b_ref, sl, row)))
        return jnp.concatenate(outs, axis=1)

    xc = conv_silu(x_ref, xh_ref, cwx_ref, cbx_ref)
    bcc = conv_silu(bc_ref, bch_ref, cwb_ref, cbb_ref)
    gn = groups * d_state
    dt = _softplus(dt_ref[...] + dtb_ref[...])
    a = dt * a_ref[...]
    acs = _dot(tt_ref[...], _hi_lo(a, 0))
    dt_full = _expand_heads(dt, ex_ref[...])
    acs_full = _expand_heads(acs, ex_ref[...])
    eacs_full = jnp.exp(acs_full)
    te_full = jnp.exp(_expand_heads(acs[lc - 1:lc, :] - acs, ex_ref[...]))
    cd_full = eacs_full[lc - 1:lc, :]
    xdt = xc * dt_full
    ti = lax.broadcasted_iota(jnp.int32, (lc, lc), 0)
    si = lax.broadcasted_iota(jnp.int32, (lc, lc), 1)
    causal = si <= ti
    lane = lax.broadcasted_iota(jnp.int32, (lc, LANES), 1)
    half_masks = (lane < HEAD_DIM, lane >= HEAD_DIM)
    pairs_per_group = di // LANES // groups
    ys = []
    for g in range(groups):
        bg = bcc[:, g * d_state:(g + 1) * d_state]
        cg = bcc[:, gn + g * d_state:gn + (g + 1) * d_state].astype(BF16)
        cb = _dotg(cg, bg.astype(BF16), NT)
        bg_t = bg.T.astype(BF16)
        for pp in range(pairs_per_group):
            p = g * pairs_per_group + pp
            sl = slice(p * LANES, (p + 1) * LANES)
            xs = xdt[:, sl]
            st = state_ref[p]
            y = _ssd_diag(acs, cb, xs, causal, half_masks, 2 * p)
            y = y + _dot(cg, st.astype(BF16)) * eacs_full[:, sl] + xc[:, sl] * d_ref[:, sl]
            state_ref[p] = st * cd_full[:, sl] + _dot(bg_t, (xs * te_full[:, sl]).astype(BF16))
            ys.append(y)
    y = jnp.concatenate(ys, axis=1)
    yz = y * _silu(z_ref[...].astype(F32))
    yg_ref[...] = _rms(yz, nw_ref[...]).astype(BF16)

    @pl.when(c_idx == pl.num_programs(1) - 1)
    def _():
        st_ref[0] = state_ref[...]


def _ssd_prompt(zx, dt, cwx, cbx, cwb, cbb, dtb, a, dfull, nw, tt, ex2, batch, seq_len, chunk,
                groups, d_state):
    n = zx.shape[0]
    di = dfull.shape[1]
    nc = seq_len // chunk
    hb = chunk // SUBLANES
    pairs = di // LANES
    kern = functools.partial(_ssd_prompt_kernel, groups=groups, d_state=d_state, lane_chunk=512)
    cur = lambda b, c: b * nc + c
    halo = lambda b, c: jnp.maximum((b * nc + c) * hb - 1, 0)
    full = lambda arr: pl.BlockSpec(arr.shape, lambda b, c: (0, 0))
    return pl.pallas_call(
        kern,
        out_shape=(jax.ShapeDtypeStruct((n, di), BF16),
                   jax.ShapeDtypeStruct((batch, pairs, d_state, LANES), F32)),
        grid=(batch, nc),
        in_specs=[pl.BlockSpec((chunk, di), lambda b, c: (cur(b, c), 0)),
                  pl.BlockSpec((chunk, di), lambda b, c: (cur(b, c), 1)),
                  pl.BlockSpec((chunk, di), lambda b, c: (cur(b, c), 2)),
                  pl.BlockSpec((SUBLANES, di), lambda b, c: (halo(b, c), 1)),
                  pl.BlockSpec((SUBLANES, di), lambda b, c: (halo(b, c), 2)),
                  pl.BlockSpec((chunk, LANES), lambda b, c: (cur(b, c), 0)),
                  full(cwx), full(cbx), full(cwb), full(cbb), full(dtb), full(a), full(dfull),
                  full(nw), full(tt), full(ex2)],
        out_specs=(pl.BlockSpec((chunk, di), lambda b, c: (cur(b, c), 0)),
                   pl.BlockSpec((1, pairs, d_state, LANES), lambda b, c: (b, 0, 0, 0))),
        scratch_shapes=[pltpu.VMEM((pairs, d_state, LANES), F32)],
        compiler_params=_params("parallel", "arbitrary"),
        name="ssd_prompt",
    )(zx, zx, zx, zx, zx, dt, cwx, cbx, cwb, cbb, dtb, a, dfull, nw, tt, ex2)


def _ssd_sample_kernel(z_ref, x_ref, bc_ref, hx_ref, hbc_ref, dt_ref, cwx_ref, cbx_ref, cwb_ref,
                       cbb_ref, dtb_ref, a_ref, d_ref, nw_ref, ex_ref, sel_ref, st_in_ref,
                       yg_ref, st_out_ref, c_s, b_s, xw_s, ea_s, y_s,
                       *, groups, d_state, tok0, lane_chunk):
    rows, di = x_ref.shape
    bi = pl.program_id(1)
    gn = groups * d_state
    pairs_per_group = di // LANES // groups

    @pl.when(bi == 0)
    def _():
        slot_c = lax.broadcasted_iota(jnp.int32, (rows, lane_chunk), 0) % SLOTS

        def conv_silu(cur_ref, hist_ref, w_ref, b_ref):
            outs = []
            for c0 in range(0, cur_ref.shape[1], lane_chunk):
                sl = slice(c0, c0 + lane_chunk)
                ext = jnp.where(slot_c < tok0, hist_ref[:, sl], cur_ref[:, sl].astype(F32))
                outs.append(_silu(_conv_taps_slots(ext, w_ref, b_ref, sl)))
            return jnp.concatenate(outs, axis=1)

        xc = conv_silu(x_ref, hx_ref, cwx_ref, cbx_ref)
        bcc = conv_silu(bc_ref, hbc_ref, cwb_ref, cbb_ref)
        slot = lax.broadcasted_iota(jnp.int32, (rows, LANES), 0) % SLOTS
        dt = jnp.where(slot >= tok0, _softplus(dt_ref[...] + dtb_ref[...]), 0.0)
        a = dt * a_ref[...]
        acs, rev = a, jnp.zeros_like(a)
        for d in range(1, SLOTS - tok0):
            acs = acs + jnp.where(slot - d >= tok0, pltpu.roll(a, d, 0), 0.0)
            rev = rev + jnp.where(slot + d < SLOTS, pltpu.roll(a, rows - d, 0), 0.0)
        dt_full = _expand_heads(dt, ex_ref[...])
        acs_full = _expand_heads(acs, ex_ref[...])
        te_full = jnp.exp(_expand_heads(rev, ex_ref[...]))
        xdt = xc * dt_full
        ti = lax.broadcasted_iota(jnp.int32, (rows, rows), 0)
        si = lax.broadcasted_iota(jnp.int32, (rows, rows), 1)
        same_seq = (si <= ti) & (si // SLOTS == ti // SLOTS)
        lane = lax.broadcasted_iota(jnp.int32, (rows, LANES), 1)
        half_masks = (lane < HEAD_DIM, lane >= HEAD_DIM)
        ys = []
        for g in range(groups):
            bg = bcc[:, g * d_state:(g + 1) * d_state].astype(BF16)
            cg = bcc[:, gn + g * d_state:gn + (g + 1) * d_state].astype(BF16)
            cb = _dotg(cg, bg, NT)
            for pp in range(pairs_per_group):
                p = g * pairs_per_group + pp
                sl = slice(p * LANES, (p + 1) * LANES)
                ys.append(_ssd_diag(acs, cb, xdt[:, sl], same_seq, half_masks, 2 * p)
                          + xc[:, sl] * d_ref[:, sl])
        y_s[...] = jnp.concatenate(ys, axis=1)
        b_s[...] = bcc[:, :gn]
        c_s[...] = bcc[:, gn:]
        xw_s[...] = xdt * te_full
        ea_s[...] = jnp.exp(acs_full)

    r0 = pl.multiple_of(bi * SLOTS, SLOTS)
    pad = jnp.zeros((SLOTS, LANES), F32)
    cb_rows = c_s[pl.ds(r0, SLOTS), :]
    bb_rows = b_s[pl.ds(r0, SLOTS), :]
    xw_rows = xw_s[pl.ds(r0, SLOTS), :]
    ea_rows = ea_s[pl.ds(r0, SLOTS), :]
    y_off = []
    for g in range(groups):
        gs = slice(g * d_state, (g + 1) * d_state)
        cg = jnp.concatenate([cb_rows[:, gs], pad], axis=0).astype(BF16)
        bg = jnp.concatenate([bb_rows[:, gs], pad], axis=0).astype(BF16)
        for pp in range(pairs_per_group):
            p = g * pairs_per_group + pp
            sl = slice(p * LANES, (p + 1) * LANES)
            st = st_in_ref[0, p]
            y_off.append(_dotg(cg, st.astype(BF16), NT)[:SLOTS] * ea_rows[:, sl])
            decay = _dotg(_hi_lo(ea_rows[:, sl], 0), sel_ref[...], TN)
            xw = jnp.concatenate([xw_rows[:, sl], pad], axis=0).astype(BF16)
            st_out_ref[0, p] = st * decay + _dotg(xw, bg, TN)
    y_s[pl.ds(r0, SLOTS), :] = y_s[pl.ds(r0, SLOTS), :] + jnp.concatenate(y_off, axis=1)

    @pl.when(bi == pl.num_programs(1) - 1)
    def _():
        yz = y_s[...] * _silu(z_ref[...].astype(F32))
        yg_ref[...] = _rms(yz, nw_ref[...]).astype(BF16)


def _ssd_sample(zx, hist, dt, cwx, cbx, cwb, cbb, dtb, a, dfull, nw, ex2, sel, state, groups,
                d_state, tok0, *, bb=16):
    n = zx.shape[0]
    bs = state.shape[0]
    di = dfull.shape[1]
    pairs = di // LANES
    bb = min(bb, bs)
    rows = bb * SLOTS
    gn = groups * d_state
    kern = functools.partial(_ssd_sample_kernel, groups=groups, d_state=d_state, tok0=tok0,
                             lane_chunk=512)
    full = lambda arr: pl.BlockSpec(arr.shape, lambda i, j: (0, 0))
    st_spec = pl.BlockSpec((1, pairs, LANES, d_state), lambda i, j: (i * bb + j, 0, 0, 0))
    return pl.pallas_call(
        kern,
        out_shape=(jax.ShapeDtypeStruct((n, di), BF16),
                   jax.ShapeDtypeStruct(state.shape, F32)),
        grid=(bs // bb, bb),
        in_specs=[pl.BlockSpec((rows, di), lambda i, j: (i, 0)),
                  pl.BlockSpec((rows, di), lambda i, j: (i, 1)),
                  pl.BlockSpec((rows, di), lambda i, j: (i, 2)),
                  pl.BlockSpec((rows, di), lambda i, j: (i, 0)),
                  pl.BlockSpec((rows, di), lambda i, j: (i, 1)),
                  pl.BlockSpec((rows, LANES), lambda i, j: (i, 0)),
                  full(cwx), full(cbx), full(cwb), full(cbb), full(dtb), full(a), full(dfull),
                  full(nw), full(ex2), full(sel), st_spec],
        out_specs=(pl.BlockSpec((rows, di), lambda i, j: (i, 0)), st_spec),
        scratch_shapes=[pltpu.VMEM((rows, gn), F32), pltpu.VMEM((rows, gn), F32),
                        pltpu.VMEM((rows, di), F32), pltpu.VMEM((rows, di), F32),
                        pltpu.VMEM((rows, di), F32)],
        compiler_params=_params("parallel", "arbitrary"),
        name="ssd_sample",
    )(zx, zx, zx, hist, hist, dt, cwx, cbx, cwb, cbb, dtb, a, dfull, nw, ex2, sel, state)


def _widen_heads(w, halves):
    d = w.shape[0]
    h = len(halves)
    onehot = jax.nn.one_hot(jnp.asarray(halves), 2, dtype=w.dtype)
    wide = w.reshape(d, h, 1, HEAD_DIM) * onehot[None, :, :, None]
    return wide.reshape(d, h * LANES)


def _suffix_sum_matrix(n, sign):
    j = jnp.arange(n)[:, None]
    s = jnp.arange(n)[None, :]
    u = sign * jnp.concatenate([(j >= s).astype(F32), jnp.ones((n, LANES), F32)], axis=1)
    return jnp.concatenate([u, u], axis=0).astype(BF16)


def _pad_slots(a, lead):
    k = a.shape[-2]
    pad = [(0, 0)] * (a.ndim - 2) + [(lead, SLOTS - lead - k), (0, 0)]
    a = jnp.pad(a, pad)
    return a.reshape(a.shape[:-3] + (a.shape[-3] * SLOTS, a.shape[-1]))


def kernel(x_prompt, x_sample, mem_prompt, cache_swa_k, cache_swa_v, state_ssm, state_ssm_conv, cache_sb_k, cache_sb_v, cache_mem_k, cache_mem_v, state_ffn_conv, page_table, norm_mix_w, norm_xattn_w, norm_ffn_w, final_norm_w, mem_token_norm_w, swa_wqkv, swa_sinks, swa_wo, ssm_w_in, ssm_conv_w, ssm_conv_b, ssm_dt_bias, ssm_a_log, ssm_d, ssm_norm_w, ssm_w_out, sb_wqkv, sb_logit_bias, sb_wo, mem_wq, mem_wkv, mem_wo, ffn_w_in, ffn_conv_w, ffn_conv_b, ffn_w_out):
    bp, seq, d = x_prompt.shape
    bs, n_tok, _ = x_sample.shape
    depth = norm_mix_w.shape[0]
    tok0 = SLOTS - n_tok
    scale = 1.0 / math.sqrt(HEAD_DIM)
    row = lambda v: v.reshape(1, -1).astype(F32)

    window, swa_kvh = cache_swa_k.shape[2], cache_swa_k.shape[3]
    swa_heads = swa_sinks.shape[1]
    swa_group = swa_heads // swa_kvh
    swa_kvw = swa_kvh * HEAD_DIM
    ssm_heads, ssm_p, d_state = state_ssm.shape[2], state_ssm.shape[3], state_ssm.shape[4]
    d_inner = ssm_heads * ssm_p
    conv_dim = ssm_conv_w.shape[2]
    gn = (conv_dim - d_inner) // 2
    groups = gn // d_state
    ssm_k = ssm_conv_w.shape[1]
    sb_heads = sb_logit_bias.shape[1]
    sb_w = sb_heads * HEAD_DIM
    page_size = cache_sb_k.shape[2]
    n_mem, mem_heads = cache_mem_k.shape[2], cache_mem_k.shape[3]
    mem_w = mem_heads * HEAD_DIM
    d_ff = ffn_conv_w.shape[2]
    ffn_k = ffn_conv_w.shape[1]
    assert ssm_p == HEAD_DIM and d_state == LANES and ssm_heads <= LANES
    assert n_tok + max(ssm_k, ffn_k) - 1 <= SLOTS and page_size == LANES and 2 * gn == d_inner
    chunk = math.gcd(seq, 128)

    sb_tq = next(t for t in (512, 256, 128) if seq % t == 0)
    sb_tk = min(256, sb_tq)
    kk = jnp.arange(sb_tk)
    uu_blk = -(kk[:, None] >= kk[None, :]).astype(BF16)
    uu_page = _suffix_sum_matrix(page_size, 1.0)
    sb_head_mask = (jnp.arange(sb_w)[None, :] // HEAD_DIM == jnp.arange(sb_heads)[:, None]).astype(F32)
    tri = (jnp.arange(chunk)[None, :] <= jnp.arange(chunk)[:, None]).astype(BF16)
    tt = jnp.concatenate([tri, tri], axis=1)
    ex = (jnp.arange(d_inner)[None, :] // ssm_p == jnp.arange(LANES)[:, None]).astype(BF16)
    ex2 = jnp.concatenate([ex, ex], axis=0)
    sel = jnp.zeros((2 * SLOTS, d_state), F32).at[SLOTS - 1].set(1.0).at[2 * SLOTS - 1].set(1.0)
    sel = sel.astype(BF16)

    xp = x_prompt.reshape(bp * seq, d)
    xs = _pad_slots(x_sample, tok0)

    wkv_all = jnp.transpose(mem_wkv, (1, 0, 2)).reshape(d, depth * 2 * mem_w).astype(BF16)
    mem_kv = _rms_linear(mem_prompt.reshape(bp * n_mem, d), row(mem_token_norm_w), wkv_all, F32,
                         name="mem_kv")
    mem_kv5 = mem_kv.reshape(bp, n_mem, depth, 2, mem_heads, HEAD_DIM)
    mem_k_prompt = jnp.moveaxis(mem_kv5[:, :, :, 0], 2, 0)
    mem_v_prompt = jnp.moveaxis(mem_kv5[:, :, :, 1], 2, 0)
    mem_kv_b = mem_kv.astype(BF16)

    ffn_hist = _pad_slots(state_ffn_conv, tok0 - (ffn_k - 1))
    cache_t = lambda c: jnp.transpose(c, (0, 1, 3, 4, 2)).reshape(depth * bs, mem_w, n_mem)
    mem_cache_kt, mem_cache_vt = cache_t(cache_mem_k), cache_t(cache_mem_v)

    outs = {k: [] for k in ("swa_kp", "swa_vp", "swa_ks", "swa_vs", "ssm_sp", "ssm_cp", "ssm_ss",
                            "ssm_cs", "sb_kp", "sb_vp", "sb_ks", "sb_vs", "ffn_cp", "ffn_cs")}

    for i in range(depth):
        kind, j = i % 3, i // 3
        nw_mix = row(norm_mix_w[i])
        if kind == 0:
            halves = [(h // swa_group) % 2 for h in range(swa_heads)]
            qw = swa_heads * HEAD_DIM
            w_all = jnp.concatenate([_widen_heads(swa_wqkv[j][:, :qw] * scale, halves),
                                     swa_wqkv[j][:, qw:]], axis=1).astype(BF16)
            order = [(2 * p + c) * swa_group + r for p in range(swa_kvh // 2)
                     for r in range(swa_group) for c in range(2)]
            wo = swa_wo[j].reshape(swa_heads, HEAD_DIM, d)[jnp.asarray(order)].reshape(qw, d).astype(BF16)
            slopes = jnp.exp2(-8.0 * jnp.arange(1, swa_heads + 1, dtype=F32) / swa_heads)
            sm = jnp.stack([slopes, swa_sinks[j].astype(F32)])
            kcol = swa_heads * LANES

            qkv_p = _rms_linear(xp, nw_mix, w_all, BF16, name="swa_qkv_prompt")
            o_p = _swa_prompt(qkv_p, sm, bp, seq, window, swa_heads, swa_group)
            xp = _linear_res(o_p, wo, xp, name="swa_out_prompt")
            kv_p = qkv_p.reshape(bp, seq, -1)[:, seq - window:, kcol:].astype(F32)
            outs["swa_kp"].append(kv_p[..., :swa_kvw].reshape(bp, window, swa_kvh, HEAD_DIM))
            outs["swa_vp"].append(kv_p[..., swa_kvw:].reshape(bp, window, swa_kvh, HEAD_DIM))

            qkv_s = _rms_linear(xs, nw_mix, w_all, BF16, name="swa_qkv_sample")
            o_s = _swa_sample(qkv_s, sm, cache_swa_k[j].reshape(bs, window, swa_kvw),
                              cache_swa_v[j].reshape(bs, window, swa_kvw), swa_heads, swa_group, tok0)
            xs = _linear_res(o_s, wo, xs, name="swa_out_sample")
            kv_s = qkv_s.reshape(bs, SLOTS, -1)[:, tok0:, kcol:].astype(F32)
            k_new = kv_s[..., :swa_kvw].reshape(bs, n_tok, swa_kvh, HEAD_DIM)
            v_new = kv_s[..., swa_kvw:].reshape(bs, n_tok, swa_kvh, HEAD_DIM)
            outs["swa_ks"].append(jnp.concatenate([cache_swa_k[j][:, n_tok:], k_new], axis=1))
            outs["swa_vs"].append(jnp.concatenate([cache_swa_v[j][:, n_tok:], v_new], axis=1))
        elif kind == 1:
            zxw = 2 * d_inner + 2 * gn
            w_main = ssm_w_in[j][:, :zxw].astype(BF16)
            w_dt = jnp.pad(ssm_w_in[j][:, zxw:], ((0, 0), (0, LANES - ssm_heads))).astype(BF16)
            pad_h = lambda v: jnp.pad(v.astype(F32), (0, LANES - ssm_heads)).reshape(1, LANES)
            cwx, cwb = ssm_conv_w[j][:, :d_inner], ssm_conv_w[j][:, d_inner:]
            cbx, cbb = row(ssm_conv_b[j][:d_inner]), row(ssm_conv_b[j][d_inner:])
            dtb = pad_h(ssm_dt_bias[j])
            a_neg = pad_h(-jnp.exp(ssm_a_log[j].astype(F32)))
            dfull = row(jnp.repeat(ssm_d[j].astype(F32), ssm_p))
            nw_ssm = row(ssm_norm_w[j])
            w_out = ssm_w_out[j].astype(BF16)

            zx_p = _rms_linear(xp, nw_mix, w_main, BF16, name="ssm_in_prompt")
            dt_p = _rms_linear(xp, nw_mix, w_dt, F32, name="ssm_dt_prompt")
            yg_p, st_p = _ssd_prompt(zx_p, dt_p, cwx, cbx, cwb, cbb, dtb, a_neg, dfull, nw_ssm, tt,
                                     ex2, bp, seq, chunk, groups, d_state)
            xp = _linear_res(yg_p, w_out, xp, name="ssm_out_prompt")
            outs["ssm_sp"].append(jnp.swapaxes(st_p, 2, 3).reshape(bp, ssm_heads, ssm_p, d_state))
            outs["ssm_cp"].append(
                zx_p.reshape(bp, seq, -1)[:, seq - (ssm_k - 1):, d_inner:].astype(F32))

            zx_s = _rms_linear(xs, nw_mix, w_main, BF16, name="ssm_in_sample")
            dt_s = _rms_linear(xs, nw_mix, w_dt, F32, name="ssm_dt_sample")
            hist = _pad_slots(state_ssm_conv[j], tok0 - (ssm_k - 1))
            yg_s, st_s = _ssd_sample(zx_s, hist, dt_s, cwx, cbx, cwb, cbb, dtb, a_neg, dfull, nw_ssm,
                                     ex2, sel, state_ssm[j].reshape(bs, d_inner // LANES, LANES, d_state),
                                     groups, d_state, tok0)
            xs = _linear_res(yg_s, w_out, xs, name="ssm_out_sample")
            outs["ssm_ss"].append(st_s.reshape(bs, ssm_heads, ssm_p, d_state))
            ext = jnp.concatenate(
                [state_ssm_conv[j], zx_s.reshape(bs, SLOTS, -1)[:, tok0:, d_inner:].astype(F32)], axis=1)
            outs["ssm_cs"].append(ext[:, -(ssm_k - 1):])
        else:
            wq, wk, wv = (sb_wqkv[j][:, t * sb_w:(t + 1) * sb_w] for t in range(3))
            w_p = jnp.concatenate([_widen_heads(wq * (scale * LOG2E), [h % 2 for h in range(sb_heads)]),
                                   wk, wv], axis=1).astype(BF16)
            w_s = jnp.concatenate([wq * scale, wk, wv], axis=1).astype(BF16)
            bias = sb_logit_bias[j].astype(F32)
            wo = sb_wo[j].astype(BF16)
            kcol = sb_heads * LANES

            qkv_p = _rms_linear(xp, nw_mix, w_p, BF16, name="sb_qkv_prompt")
            o_p = _sb_prompt(qkv_p, bias * LOG2E, uu_blk, bp, seq, sb_heads, sb_tq, sb_tk)
            xp = _linear_res(o_p, wo, xp, name="sb_out_prompt")
            kv_p = qkv_p[:, kcol:].astype(F32)
            outs["sb_kp"].append(kv_p[:, :sb_w].reshape(bp, seq, sb_heads, HEAD_DIM))
            outs["sb_vp"].append(kv_p[:, sb_w:].reshape(bp, seq, sb_heads, HEAD_DIM))

            qkv_s = _rms_linear(xs, nw_mix, w_s, F32, name="sb_qkv_sample")
            n_c, n_pool = cache_sb_k.shape[:2]
            pages_t = lambda c: jnp.transpose(c, (0, 1, 3, 4, 2)).reshape(n_c * n_pool, sb_w, page_size)
            o_s = _sb_sample(qkv_s, pages_t(cache_sb_k), pages_t(cache_sb_v), j * n_pool, page_table,
                             jnp.tile(bias, n_tok).reshape(-1, 1), sb_head_mask, uu_page,
                             sb_heads, n_tok, tok0)
            xs = _linear_res(o_s, wo, xs, name="sb_out_sample")
            kv_s = qkv_s.reshape(bs, SLOTS, -1)[:, tok0:, sb_w:]
            outs["sb_ks"].append(kv_s[..., :sb_w].reshape(bs, n_tok, sb_heads, HEAD_DIM))
            outs["sb_vs"].append(kv_s[..., sb_w:].reshape(bs, n_tok, sb_heads, HEAD_DIM))

        nw_x = row(norm_xattn_w[i])
        wq_m = _widen_heads(mem_wq[i] * scale, [h % 2 for h in range(mem_heads)]).astype(BF16)
        wo_m = mem_wo[i].astype(BF16)
        k_m = mem_kv_b[:, (2 * i) * mem_w:(2 * i + 1) * mem_w]
        v_m = mem_kv_b[:, (2 * i + 1) * mem_w:(2 * i + 2) * mem_w]
        xp = _mem_attn_prompt(xp, nw_x, wq_m, k_m, v_m, wo_m, seq, n_mem)
        xs = _mem_attn_sample(xs, nw_x, wq_m, mem_cache_kt, mem_cache_vt, wo_m, i)

        nw_f = row(norm_ffn_w[i])
        w_in = ffn_w_in[i].astype(BF16)
        w_out = ffn_w_out[i].astype(BF16)
        cw, cb = ffn_conv_w[i].astype(F32), row(ffn_conv_b[i])
        gu_p = _rms_linear(xp, nw_f, w_in, BF16, name="ffn_in_prompt")
        xp = _glu_prompt(gu_p, cw, cb, w_out, xp, seq)
        outs["ffn_cp"].append(gu_p.reshape(bp, seq, -1)[:, seq - (ffn_k - 1):, :d_ff].astype(F32))
        gu_s = _rms_linear(xs, nw_f, w_in, BF16, name="ffn_in_sample")
        xs = _glu_sample(gu_s, ffn_hist[i], cw, cb, w_out, xs, tok0)
        ext = jnp.concatenate(
            [state_ffn_conv[i], gu_s.reshape(bs, SLOTS, -1)[:, tok0:, :d_ff].astype(F32)], axis=1)
        outs["ffn_cs"].append(ext[:, -(ffn_k - 1):])

    y_prompt = _rmsnorm(xp, row(final_norm_w)).reshape(bp, seq, d)
    y_sample = _rmsnorm(xs, row(final_norm_w)).reshape(bs, SLOTS, d)[:, tok0:]
    st = lambda k: jnp.stack(outs[k])
    return (y_prompt, y_sample, st("swa_kp"), st("swa_vp"), st("swa_ks"), st("swa_vs"),
            st("ssm_sp"), st("ssm_cp"), st("ssm_ss"), st("ssm_cs"),
            st("sb_kp"), st("sb_vp"), st("sb_ks"), st("sb_vs"),
            mem_k_prompt, mem_v_prompt, st("ffn_cp"), st("ffn_cs"))
```

```python
import functools
import math

import jax
import jax.numpy as jnp
from jax import lax
from jax.experimental import pallas as pl
from jax.experimental.pallas import tpu as pltpu

F32 = jnp.float32
BF16 = jnp.bfloat16
EPS = 1e-6
HEAD_DIM = 64
LANES = 128
SUBLANES = 8
SLOTS = SUBLANES
NEG = -1e30
VMEM_LIMIT = 56 * 1024 * 1024
LOG2E = 1.0 / math.log(2.0)
NT = (((1,), (1,)), ((), ()))
TN = (((0,), (0,)), ((), ()))


def _params(*sem):
    return pltpu.CompilerParams(dimension_semantics=sem, vmem_limit_bytes=VMEM_LIMIT)


def _tile(n, want):
    if n <= want:
        return n
    t = (want // LANES) * LANES
    while t > LANES and n % t:
        t -= LANES
    assert n % t == 0, (n, want)
    return t


def _dot(a, b):
    return jnp.dot(a, b, preferred_element_type=F32)


def _dotg(a, b, dims):
    return lax.dot_general(a, b, dims, preferred_element_type=F32)


def _hi_lo(v, axis):
    hi = v.astype(BF16).astype(F32)
    return jnp.concatenate([hi, v - hi], axis=axis).astype(BF16)


def _softplus(z):
    return jnp.maximum(z, 0.0) + jnp.log(1.0 + jnp.exp(-jnp.abs(z)))


def _silu(v):
    return v * jax.nn.sigmoid(v)


def _rms(x, w):
    ms = jnp.mean(x * x, axis=-1, keepdims=True)
    return x * lax.rsqrt(ms + EPS) * w


def _rms_linear_kernel(x_ref, nw_ref, w_ref, o_ref, xn_ref, *, tn):
    xn_ref[...] = _rms(x_ref[...], nw_ref[...]).astype(BF16)
    for c0 in range(0, w_ref.shape[1], tn):
        o_ref[:, c0:c0 + tn] = _dot(xn_ref[...], w_ref[:, c0:c0 + tn]).astype(o_ref.dtype)


def _rms_linear(x, nw, w, out_dtype, *, tm=512, tn=512, name="rms_linear"):
    n, d = x.shape
    f = w.shape[1]
    tm, tn = _tile(n, tm), _tile(f, tn)
    return pl.pallas_call(
        functools.partial(_rms_linear_kernel, tn=tn),
        out_shape=jax.ShapeDtypeStruct((n, f), out_dtype),
        grid=(n // tm,),
        in_specs=[pl.BlockSpec((tm, d), lambda i: (i, 0)),
                  pl.BlockSpec((1, d), lambda i: (0, 0)),
                  pl.BlockSpec((d, f), lambda i: (0, 0), pipeline_mode=pl.Buffered(1))],
        out_specs=pl.BlockSpec((tm, f), lambda i: (i, 0)),
        scratch_shapes=[pltpu.VMEM((tm, d), BF16)],
        compiler_params=_params("parallel"),
        name=name,
    )(x, nw, w)


def _rms_linear_t_kernel(x_ref, nw_ref, w_ref, wt_ref, o_ref, *rest, tn, n_t):
    t_refs, xn_ref = rest[:n_t], rest[n_t]
    xn_ref[...] = _rms(x_ref[...], nw_ref[...]).astype(BF16)
    for c0 in range(0, w_ref.shape[1], tn):
        o_ref[:, c0:c0 + tn] = _dot(xn_ref[...], w_ref[:, c0:c0 + tn]).astype(o_ref.dtype)
    rows = t_refs[0].shape[0]
    for t, t_ref in enumerate(t_refs):
        for r0 in range(0, rows, 256):
            wt = wt_ref[t * rows + r0:t * rows + r0 + 256, :]
            t_ref[r0:r0 + 256, :] = _dotg(wt, xn_ref[...], NT)


def _rms_linear_t(x, nw, w, wt, n_t, seq_len, *, tm=512, tn=512, name="rms_linear_t"):
    n, d = x.shape
    f = w.shape[1]
    rows = wt.shape[0] // n_t
    tm, tn = _tile(seq_len, tm), _tile(f, tn)
    tps = seq_len // tm
    t_shape = jax.ShapeDtypeStruct((n // seq_len, rows, seq_len), F32)
    t_spec = pl.BlockSpec((None, rows, tm), lambda i: (i // tps, 0, i % tps))
    return pl.pallas_call(
        functools.partial(_rms_linear_t_kernel, tn=tn, n_t=n_t),
        out_shape=(jax.ShapeDtypeStruct((n, f), BF16),) + (t_shape,) * n_t,
        grid=(n // tm,),
        in_specs=[pl.BlockSpec((tm, d), lambda i: (i, 0)),
                  pl.BlockSpec((1, d), lambda i: (0, 0)),
                  pl.BlockSpec((d, f), lambda i: (0, 0), pipeline_mode=pl.Buffered(1)),
                  pl.BlockSpec(wt.shape, lambda i: (0, 0), pipeline_mode=pl.Buffered(1))],
        out_specs=(pl.BlockSpec((tm, f), lambda i: (i, 0)),) + (t_spec,) * n_t,
        scratch_shapes=[pltpu.VMEM((tm, d), BF16)],
        compiler_params=_params("parallel"),
        name=name,
    )(x, nw, w, wt)


def _linear_res_kernel(y_ref, w_ref, r_ref, o_ref):
    o_ref[...] = r_ref[...] + _dot(y_ref[...].astype(BF16), w_ref[...])


def _linear_res(y, w, res, *, tm=512, name="linear_res"):
    n, k = y.shape
    d = w.shape[1]
    tm = _tile(n, tm)
    return pl.pallas_call(
        _linear_res_kernel,
        out_shape=jax.ShapeDtypeStruct((n, d), F32),
        grid=(n // tm,),
        in_specs=[pl.BlockSpec((tm, k), lambda i: (i, 0)),
                  pl.BlockSpec((k, d), lambda i: (0, 0)),
                  pl.BlockSpec((tm, d), lambda i: (i, 0))],
        out_specs=pl.BlockSpec((tm, d), lambda i: (i, 0)),
        compiler_params=_params("parallel"),
        name=name,
    )(y, w, res)


def _rmsnorm_kernel(x_ref, nw_ref, o_ref):
    o_ref[...] = _rms(x_ref[...], nw_ref[...])


def _rmsnorm(x, nw, *, tm=512):
    n, d = x.shape
    tm = _tile(n, tm)
    return pl.pallas_call(
        _rmsnorm_kernel,
        out_shape=jax.ShapeDtypeStruct((n, d), F32),
        grid=(n // tm,),
        in_specs=[pl.BlockSpec((tm, d), lambda i: (i, 0)),
                  pl.BlockSpec((1, d), lambda i: (0, 0))],
        out_specs=pl.BlockSpec((tm, d), lambda i: (i, 0)),
        compiler_params=_params("parallel"),
        name="final_rmsnorm",
    )(x, nw)


def _shift_rows(cur, halo, d, row):
    if d == 0:
        return cur
    out = pltpu.roll(cur, d, 0)
    top = jnp.where(row < d, pltpu.roll(halo, d, 0), out[:SUBLANES])
    return jnp.concatenate([top, out[SUBLANES:]], axis=0)


def _conv_taps(cur, halo, w_ref, b_ref, sl, row):
    k = w_ref.shape[0]
    acc = b_ref[:, sl] + w_ref[k - 1:k, sl] * cur
    for d in range(1, k):
        acc = acc + w_ref[k - 1 - d:k - d, sl] * _shift_rows(cur, halo, d, row)
    return acc


def _conv_taps_slots(ext, w_ref, b_ref, sl):
    k = w_ref.shape[0]
    acc = b_ref[:, sl] + w_ref[k - 1:k, sl] * ext
    for d in range(1, k):
        acc = acc + w_ref[k - 1 - d:k - d, sl] * pltpu.roll(ext, d, 0)
    return acc


def _glu_prompt_kernel(g_ref, u_ref, halo_ref, cw_ref, cb_ref, w_ref, r_ref, o_ref, h_ref,
                       *, tiles_per_seq, lane_chunk):
    tm, f = g_ref.shape
    not_first = (pl.program_id(0) % tiles_per_seq) != 0
    row = lax.broadcasted_iota(jnp.int32, (SUBLANES, lane_chunk), 0)
    for c0 in range(0, f, lane_chunk):
        sl = slice(c0, c0 + lane_chunk)
        g = g_ref[:, sl].astype(F32)
        halo = jnp.where(not_first, halo_ref[:, sl].astype(F32), 0.0)
        conv = _conv_taps(g, halo, cw_ref, cb_ref, sl, row)
        h_ref[:, sl] = (_silu(conv) * u_ref[:, sl].astype(F32)).astype(BF16)
    o_ref[...] = r_ref[...] + _dot(h_ref[...], w_ref[...])


def _glu_prompt(gu, cw, cb, w_out, res, seq_len, *, tm=512):
    n, f2 = gu.shape
    f = f2 // 2
    d = w_out.shape[1]
    tm = _tile(seq_len, tm)
    hb = tm // SUBLANES
    lane_chunk = 256 if f % 256 == 0 else LANES
    kern = functools.partial(_glu_prompt_kernel, tiles_per_seq=seq_len // tm, lane_chunk=lane_chunk)
    return pl.pallas_call(
        kern,
        out_shape=jax.ShapeDtypeStruct((n, d), F32),
        grid=(n // tm,),
        in_specs=[pl.BlockSpec((tm, f), lambda i: (i, 0)),
                  pl.BlockSpec((tm, f), lambda i: (i, 1)),
                  pl.BlockSpec((SUBLANES, f), lambda i: (jnp.maximum(i * hb - 1, 0), 0)),
                  pl.BlockSpec(cw.shape, lambda i: (0, 0)),
                  pl.BlockSpec((1, f), lambda i: (0, 0)),
                  pl.BlockSpec((f, d), lambda i: (0, 0)),
                  pl.BlockSpec((tm, d), lambda i: (i, 0))],
        out_specs=pl.BlockSpec((tm, d), lambda i: (i, 0)),
        scratch_shapes=[pltpu.VMEM((tm, f), BF16)],
        compiler_params=_params("parallel"),
        name="glu_prompt",
    )(gu, gu, gu, cw, cb, w_out, res)


def _glu_sample_kernel(g_ref, u_ref, hist_ref, cw_ref, cb_ref, w_ref, r_ref, o_ref, h_ref,
                       *, tok0, lane_chunk):
    tm, f = g_ref.shape
    row = lax.broadcasted_iota(jnp.int32, (tm, lane_chunk), 0)
    is_hist = (row % SLOTS) < tok0
    for c0 in range(0, f, lane_chunk):
        sl = slice(c0, c0 + lane_chunk)
        ext = jnp.where(is_hist, hist_ref[:, sl], g_ref[:, sl].astype(F32))
        conv = _conv_taps_slots(ext, cw_ref, cb_ref, sl)
        h_ref[:, sl] = (_silu(conv) * u_ref[:, sl].astype(F32)).astype(BF16)
    o_ref[...] = r_ref[...] + _dot(h_ref[...], w_ref[...])


def _glu_sample(gu, hist, cw, cb, w_out, res, tok0, *, tm=256):
    n, f2 = gu.shape
    f = f2 // 2
    d = w_out.shape[1]
    tm = _tile(n, tm)
    lane_chunk = 256 if f % 256 == 0 else LANES
    kern = functools.partial(_glu_sample_kernel, tok0=tok0, lane_chunk=lane_chunk)
    return pl.pallas_call(
        kern,
        out_shape=jax.ShapeDtypeStruct((n, d), F32),
        grid=(n // tm,),
        in_specs=[pl.BlockSpec((tm, f), lambda i: (i, 0)),
                  pl.BlockSpec((tm, f), lambda i: (i, 1)),
                  pl.BlockSpec((tm, f), lambda i: (i, 0)),
                  pl.BlockSpec(cw.shape, lambda i: (0, 0)),
                  pl.BlockSpec((1, f), lambda i: (0, 0)),
                  pl.BlockSpec((f, d), lambda i: (0, 0)),
                  pl.BlockSpec((tm, d), lambda i: (i, 0))],
        out_specs=pl.BlockSpec((tm, d), lambda i: (i, 0)),
        scratch_shapes=[pltpu.VMEM((tm, f), BF16)],
        compiler_params=_params("parallel"),
        name="glu_sample",
    )(gu, gu, hist, cw, cb, w_out, res)


def _softmax_pv(s, v, transposed):
    m = jnp.max(s, axis=-1, keepdims=True)
    p = jnp.exp(s - m)
    l = jnp.sum(p, axis=-1, keepdims=True)
    pv = _dotg(p.astype(BF16), v, NT) if transposed else _dot(p.astype(BF16), v)
    return pv / l


def _mem_heads(q, k, v, n_heads, transposed=False):
    half0 = lax.broadcasted_iota(jnp.int32, (q.shape[0], LANES), 1) < HEAD_DIM
    slabs = []
    for s in range(n_heads // 2):
        sl = slice(s * LANES, (s + 1) * LANES)
        ks, vs = (k[sl, :], v[sl, :]) if transposed else (k[:, sl], v[:, sl])
        o = []
        for c in range(2):
            qc = q[:, (2 * s + c) * LANES:(2 * s + c + 1) * LANES]
            scores = _dot(qc, ks) if transposed else _dotg(qc, ks, NT)
            o.append(_softmax_pv(scores, vs, transposed))
        slabs.append(jnp.where(half0, o[0], o[1]))
    return jnp.concatenate(slabs, axis=1)


def _mem_attn_prompt_kernel(x_ref, nw_ref, wq_ref, k_ref, v_ref, wo_ref, o_ref, *, n_heads):
    x = x_ref[...]
    q = _dot(_rms(x, nw_ref[...]).astype(BF16), wq_ref[...]).astype(BF16)
    o = _mem_heads(q, k_ref[...], v_ref[...], n_heads)
    o_ref[...] = x + _dot(o.astype(BF16), wo_ref[...])


def _mem_attn_prompt(x, nw, wq, k, v, wo, seq_len, n_mem, *, tm=512):
    n, d = x.shape
    hw = k.shape[1]
    tm = _tile(seq_len, tm)
    tps = seq_len // tm
    kern = functools.partial(_mem_attn_prompt_kernel, n_heads=hw // HEAD_DIM)
    return pl.pallas_call(
        kern,
        out_shape=jax.ShapeDtypeStruct((n, d), F32),
        grid=(n // tm,),
        in_specs=[pl.BlockSpec((tm, d), lambda i: (i, 0)),
                  pl.BlockSpec((1, d), lambda i: (0, 0)),
                  pl.BlockSpec(wq.shape, lambda i: (0, 0)),
                  pl.BlockSpec((n_mem, hw), lambda i: (i // tps, 0)),
                  pl.BlockSpec((n_mem, hw), lambda i: (i // tps, 0)),
                  pl.BlockSpec(wo.shape, lambda i: (0, 0))],
        out_specs=pl.BlockSpec((tm, d), lambda i: (i, 0)),
        compiler_params=_params("parallel"),
        name="mem_attn_prompt",
    )(x, nw, wq, k, v, wo)


def _mem_attn_sample_kernel(x_ref, nw_ref, wq_ref, k_ref, v_ref, wo_ref, o_ref, *, n_heads, bb):
    x = x_ref[...]
    q = _dot(_rms(x, nw_ref[...]).astype(BF16), wq_ref[...])
    outs = []
    for b in range(bb):
        qb = q[b * SLOTS:(b + 1) * SLOTS].astype(BF16)
        outs.append(_mem_heads(qb, k_ref[b].astype(BF16), v_ref[b].astype(BF16), n_heads,
                               transposed=True))
    o = jnp.concatenate(outs, axis=0)
    o_ref[...] = x + _dot(o.astype(BF16), wo_ref[...])


def _mem_attn_sample(x, nw, wq, k, v, wo, layer, *, bb=8):
    n, d = x.shape
    bs = n // SLOTS
    _, hw, n_mem = k.shape
    bb = min(bb, bs)
    base = layer * (bs // bb)
    kern = functools.partial(_mem_attn_sample_kernel, n_heads=hw // HEAD_DIM, bb=bb)
    return pl.pallas_call(
        kern,
        out_shape=jax.ShapeDtypeStruct((n, d), F32),
        grid=(bs // bb,),
        in_specs=[pl.BlockSpec((bb * SLOTS, d), lambda i: (i, 0)),
                  pl.BlockSpec((1, d), lambda i: (0, 0)),
                  pl.BlockSpec(wq.shape, lambda i: (0, 0)),
                  pl.BlockSpec((bb, hw, n_mem), lambda i: (base + i, 0, 0)),
                  pl.BlockSpec((bb, hw, n_mem), lambda i: (base + i, 0, 0)),
                  pl.BlockSpec(wo.shape, lambda i: (0, 0))],
        out_specs=pl.BlockSpec((bb * SLOTS, d), lambda i: (i, 0)),
        compiler_params=_params("parallel"),
        name="mem_attn_sample",
    )(x, nw, wq, k, v, wo)


def _sink_softmax_pv(s, sink, v):
    m = jnp.maximum(jnp.max(s, axis=-1, keepdims=True), sink)
    p = jnp.exp(s - m)
    l = jnp.sum(p, axis=-1, keepdims=True) + jnp.exp(sink - m)
    return _dot(p.astype(BF16), v) / l


def _swa_prompt_kernel(sm_ref, q_ref, kc_ref, vc_ref, kp_ref, vp_ref, o_ref, *, n_heads, group):
    w = q_ref.shape[0]
    n = pl.program_id(1)
    a = lax.broadcasted_iota(jnp.int32, (w, 2 * w), 0)
    j = lax.broadcasted_iota(jnp.int32, (w, 2 * w), 1)
    dist = (a + w - j).astype(F32)
    has_prev = jnp.where(n > 0, 0, w)
    bias_mask = jnp.where((j >= a) & (j <= a + w) & (j >= has_prev), 0.0, NEG)
    half0 = lax.broadcasted_iota(jnp.int32, (w, LANES), 1) < HEAD_DIM
    kv_pairs = n_heads // group // 2
    for p in range(kv_pairs):
        sl = slice(p * LANES, (p + 1) * LANES)
        kpair = jnp.concatenate([kp_ref[:, sl], kc_ref[:, sl]], axis=0)
        vpair = jnp.concatenate([vp_ref[:, sl], vc_ref[:, sl]], axis=0)
        for r in range(group):
            o = []
            for c in range(2):
                h = (2 * p + c) * group + r
                s = _dotg(q_ref[:, h * LANES:(h + 1) * LANES], kpair, NT)
                s = s - sm_ref[0, h] * dist + bias_mask
                o.append(_sink_softmax_pv(s, sm_ref[1, h], vpair))
            slab = p * group + r
            o_ref[:, slab * LANES:(slab + 1) * LANES] = jnp.where(half0, o[0], o[1]).astype(BF16)


def _swa_prompt(qkv, sm, batch, seq_len, window, n_heads, group):
    n = qkv.shape[0]
    nb = seq_len // window
    qw = n_heads * LANES
    kvw = n_heads // group * HEAD_DIM
    kcol, vcol = qw // kvw, qw // kvw + 1
    kern = functools.partial(_swa_prompt_kernel, n_heads=n_heads, group=group)
    cur = lambda b, i: b * nb + i
    prev = lambda b, i: b * nb + jnp.maximum(i - 1, 0)
    return pl.pallas_call(
        kern,
        out_shape=jax.ShapeDtypeStruct((n, n_heads * HEAD_DIM), BF16),
        grid=(batch, nb),
        in_specs=[pl.BlockSpec(memory_space=pltpu.SMEM),
                  pl.BlockSpec((window, qw), lambda b, i: (cur(b, i), 0)),
                  pl.BlockSpec((window, kvw), lambda b, i: (cur(b, i), kcol)),
                  pl.BlockSpec((window, kvw), lambda b, i: (cur(b, i), vcol)),
                  pl.BlockSpec((window, kvw), lambda b, i: (prev(b, i), kcol)),
                  pl.BlockSpec((window, kvw), lambda b, i: (prev(b, i), vcol))],
        out_specs=pl.BlockSpec((window, n_heads * HEAD_DIM), lambda b, i: (cur(b, i), 0)),
        compiler_params=_params("parallel", "parallel"),
        name="swa_prompt",
    )(sm, qkv, qkv, qkv, qkv, qkv)


def _swa_sample_kernel(sm_ref, q_ref, kn_ref, vn_ref, kc_ref, vc_ref, o_ref,
                       *, n_heads, group, bb, tok0):
    w = kc_ref.shape[1]
    rows = group * SLOTS
    row = lax.broadcasted_iota(jnp.int32, (rows, 2 * w), 0)
    j = lax.broadcasted_iota(jnp.int32, (rows, 2 * w), 1)
    tq = row % SLOTS - tok0
    in_cache = j < w
    dist_i = jnp.where(in_cache, w + tq - j, tq + tok0 + w - j)
    mask_cache = jnp.where(j >= tq, 0.0, NEG)
    mask_new = jnp.where((j >= w + tok0) & (j < w + SLOTS) & (dist_i >= 0), 0.0, NEG)
    bias_mask = jnp.where(in_cache, mask_cache, mask_new)
    dist = dist_i.astype(F32)
    rgrp = lax.broadcasted_iota(jnp.int32, (rows, 1), 0) // SLOTS
    half0 = lax.broadcasted_iota(jnp.int32, (SLOTS, LANES), 1) < HEAD_DIM
    zeros = jnp.zeros((w - SLOTS, LANES), F32)
    q_all = q_ref[...].astype(F32)
    kn_all = kn_ref[...].astype(F32)
    vn_all = vn_ref[...].astype(F32)
    kv_pairs = n_heads // group // 2
    out_rows = []
    for b in range(bb):
        rs = slice(b * SLOTS, (b + 1) * SLOTS)
        slabs = [None] * (n_heads // 2)
        for p in range(kv_pairs):
            sl = slice(p * LANES, (p + 1) * LANES)
            kfull = jnp.concatenate([kc_ref[b, :, sl], kn_all[rs, sl], zeros], axis=0).astype(BF16)
            vfull = jnp.concatenate([vc_ref[b, :, sl], vn_all[rs, sl], zeros], axis=0).astype(BF16)
            o = []
            for c in range(2):
                heads = [(2 * p + c) * group + r for r in range(group)]
                qs = jnp.concatenate([q_all[rs, h * LANES:(h + 1) * LANES] for h in heads], axis=0)
                slope = jnp.zeros((rows, 1), F32)
                sink = jnp.zeros((rows, 1), F32)
                for r, h in enumerate(heads):
                    slope = jnp.where(rgrp == r, sm_ref[0, h], slope)
                    sink = jnp.where(rgrp == r, sm_ref[1, h], sink)
                s = _dotg(qs.astype(BF16), kfull, NT) - slope * dist + bias_mask
                o.append(_sink_softmax_pv(s, sink, vfull))
            for r in range(group):
                piece = slice(r * SLOTS, (r + 1) * SLOTS)
                slabs[p * group + r] = jnp.where(half0, o[0][piece], o[1][piece])
        out_rows.append(jnp.concatenate(slabs, axis=1))
    o_ref[...] = jnp.concatenate(out_rows, axis=0).astype(BF16)


def _swa_sample(qkv, sm, kc, vc, n_heads, group, tok0, *, bb=8):
    n = qkv.shape[0]
    bs, window, kvw = kc.shape
    bb = min(bb, bs)
    qw = n_heads * LANES
    kcol, vcol = qw // kvw, qw // kvw + 1
    kern = functools.partial(_swa_sample_kernel, n_heads=n_heads, group=group, bb=bb, tok0=tok0)
    return pl.pallas_call(
        kern,
        out_shape=jax.ShapeDtypeStruct((n, n_heads * HEAD_DIM), BF16),
        grid=(bs // bb,),
        in_specs=[pl.BlockSpec(memory_space=pltpu.SMEM),
                  pl.BlockSpec((bb * SLOTS, qw), lambda i: (i, 0)),
                  pl.BlockSpec((bb * SLOTS, kvw), lambda i: (i, kcol)),
                  pl.BlockSpec((bb * SLOTS, kvw), lambda i: (i, vcol)),
                  pl.BlockSpec((bb, window, kvw), lambda i: (i, 0, 0)),
                  pl.BlockSpec((bb, window, kvw), lambda i: (i, 0, 0))],
        out_specs=pl.BlockSpec((bb * SLOTS, n_heads * HEAD_DIM), lambda i: (i, 0)),
        compiler_params=_params("parallel"),
        name="swa_sample",
    )(sm, qkv, qkv, qkv, kc, vc)


def _sb_prompt_kernel(bias_ref, q_ref, k_ref, v_ref, uu_ref, o_ref, qs_ref, brow_ref, acc_ref,
                      r_ref, *, tk):
    tq = q_ref.shape[0]
    m = 2 * tq
    pair = pl.program_id(1)
    i = pl.program_id(2)
    for c in range(2):
        qs_ref[c * tq:(c + 1) * tq, :] = q_ref[:, c * LANES:(c + 1) * LANES]
        brow_ref[c * tq:(c + 1) * tq, :] = jnp.full((tq, tk), bias_ref[2 * pair + c], F32)
    acc_ref[...] = jnp.zeros_like(acc_ref)
    r_ref[...] = jnp.zeros_like(r_ref)
    qpos = lax.broadcasted_iota(jnp.int32, (m, tk), 0) % tq + i * tq
    kcol = lax.broadcasted_iota(jnp.int32, (m, tk), 1)

    def visit(j, masked):
        start = pl.multiple_of(j * tk, tk)
        kj = k_ref[pl.ds(start, tk), :]
        vj = v_ref[pl.ds(start, tk), :]
        z = _dotg(qs_ref[...], kj, NT) + brow_ref[...]
        sp = jnp.maximum(z, 0.0) + jnp.log(1.0 + jnp.exp2(-jnp.abs(z))) * LOG2E
        if masked:
            valid = kcol + j * tk < qpos
            sp = jnp.where(valid, sp, 0.0)
        incl = _dot(sp.astype(BF16), uu_ref[...])
        r = r_ref[...]
        e = z + incl + jnp.concatenate([r] * (tk // LANES), axis=1)
        if masked:
            e = jnp.where(valid, e, NEG)
        acc_ref[...] = acc_ref[...] + _dot(jnp.exp2(e).astype(BF16), vj)
        r_ref[...] = r + incl[:, 0:1]

    per_q = tq // tk
    for u in range(per_q):
        visit((i + 1) * per_q - 1 - u, True)

    def body(jj, carry):
        for u in range(per_q):
            visit((i - jj) * per_q - 1 - u, False)
        return carry

    lax.fori_loop(0, i, body, 0)
    half0 = lax.broadcasted_iota(jnp.int32, (tq, LANES), 1) < HEAD_DIM
    o_ref[...] = jnp.where(half0, acc_ref[0:tq, :], acc_ref[tq:m, :]).astype(BF16)


def _sb_prompt(qkv, bias2, uu, batch, seq_len, n_heads, tq, tk):
    n = qkv.shape[0]
    nq = seq_len // tq
    pairs = n_heads // 2
    kcol0 = n_heads
    vcol0 = n_heads + pairs
    return pl.pallas_call(
        functools.partial(_sb_prompt_kernel, tk=tk),
        out_shape=jax.ShapeDtypeStruct((n, n_heads * HEAD_DIM), BF16),
        grid=(batch, pairs, nq),
        in_specs=[pl.BlockSpec(memory_space=pltpu.SMEM),
                  pl.BlockSpec((tq, 2 * LANES), lambda b, p, i: (b * nq + i, p)),
                  pl.BlockSpec((seq_len, LANES), lambda b, p, i: (b, kcol0 + p)),
                  pl.BlockSpec((seq_len, LANES), lambda b, p, i: (b, vcol0 + p)),
                  pl.BlockSpec(uu.shape, lambda b, p, i: (0, 0))],
        out_specs=pl.BlockSpec((tq, LANES), lambda b, p, i: (b * nq + i, p)),
        scratch_shapes=[pltpu.VMEM((2 * tq, LANES), BF16), pltpu.VMEM((2 * tq, tk), F32),
                        pltpu.VMEM((2 * tq, LANES), F32), pltpu.VMEM((2 * tq, LANES), F32)],
        compiler_params=_params("parallel", "parallel", "arbitrary"),
        name="sb_prompt",
    )(bias2, qkv, qkv, qkv, uu)


def _sb_weights(z, uu, r_prev, valid):
    kb = z.shape[1]
    lk = -_softplus(z)
    if valid is not None:
        lk = jnp.where(valid, lk, 0.0)
    both = _dot(_hi_lo(lk, 1), uu)
    e = z + both[:, :kb] + r_prev
    if valid is not None:
        e = jnp.where(valid, e, NEG)
    return jnp.exp(e).astype(BF16), both[:, kb:]


def _sb_sample_kernel(pt_ref, q_ref, kn_ref, vn_ref, bias_ref, mask_ref, uu_ref, *rest,
                      n_heads, n_tok, tok0, pages_per_step):
    k_refs = rest[:pages_per_step]
    v_refs = rest[pages_per_step:2 * pages_per_step]
    o_ref, qb_ref, acc_ref, r_ref = rest[2 * pages_per_step:]
    hw, ps = k_refs[0].shape
    step = pl.program_id(1)
    rows = n_tok * n_heads

    @pl.when(step == 0)
    def _():
        q = q_ref[...]
        for t in range(n_tok):
            qb_ref[t * n_heads:(t + 1) * n_heads, :] = (
                q[tok0 + t:tok0 + t + 1, :] * mask_ref[...]).astype(BF16)
        zeros = jnp.zeros((ps - SLOTS, hw), F32)
        kn = jnp.concatenate([kn_ref[...], zeros], axis=0).astype(BF16)
        vn = jnp.concatenate([vn_ref[...], zeros], axis=0).astype(BF16)
        tq = lax.broadcasted_iota(jnp.int32, (rows, ps), 0) // n_heads
        j = lax.broadcasted_iota(jnp.int32, (rows, ps), 1)
        a, tot = _sb_weights(_dotg(qb_ref[...], kn, NT) + bias_ref[...], uu_ref[...],
                             jnp.zeros((rows, ps), F32), (j >= tok0) & (j < tok0 + tq))
        acc_ref[...] = _dot(a, vn)
        r_ref[...] = tot

    group = 2 if pages_per_step % 2 == 0 else 1
    side_by_side = lambda refs, g: jnp.concatenate(
        [refs[g + u][...].astype(BF16) for u in range(group)], axis=1)
    zs = []
    for g in range(0, pages_per_step, group):
        z = _dot(qb_ref[...], side_by_side(k_refs, g)) + bias_ref[...]
        zs += [z[:, u * ps:(u + 1) * ps] for u in range(group)]
    boths = [_dot(_hi_lo(-_softplus(z), 1), uu_ref[...]) for z in zs]
    r = r_ref[...]
    acc = acc_ref[...]
    for g in range(0, pages_per_step, group):
        a = []
        for u in range(group):
            a.append(jnp.exp(zs[g + u] + boths[g + u][:, :ps] + r).astype(BF16))
            r = r + boths[g + u][:, ps:]
        acc = acc + _dotg(jnp.concatenate(a, axis=1), side_by_side(v_refs, g), NT)
    acc_ref[...] = acc
    r_ref[...] = r

    @pl.when(step == pl.num_programs(1) - 1)
    def _():
        o_ref[0:tok0, :] = jnp.zeros((tok0, hw), o_ref.dtype)
        for t in range(n_tok):
            blk = acc_ref[t * n_heads:(t + 1) * n_heads, :] * mask_ref[...]
            o_ref[tok0 + t:tok0 + t + 1, :] = jnp.sum(blk, axis=0, keepdims=True)


def _sb_sample(qkv, pool_k, pool_v, pool_base, page_table, bias_col, head_mask, uu, n_heads,
               n_tok, tok0, *, pages_per_step=16):
    n = qkv.shape[0]
    bs, n_pages = page_table.shape
    _, hw, ps = pool_k.shape
    g = pages_per_step
    while n_pages % g:
        g //= 2
    steps = n_pages // g
    rows = n_tok * n_heads
    pt = page_table.reshape(-1)

    def page_map(gi):
        return lambda b, s, pt_ref: (
            pool_base + pt_ref[b * n_pages + n_pages - 1 - (s * g + gi)], 0, 0)

    page_specs = [pl.BlockSpec((None, hw, ps), page_map(gi)) for gi in range(g)]
    kern = functools.partial(_sb_sample_kernel, n_heads=n_heads, n_tok=n_tok, tok0=tok0,
                             pages_per_step=g)
    grid_spec = pltpu.PrefetchScalarGridSpec(
        num_scalar_prefetch=1,
        grid=(bs, steps),
        in_specs=[pl.BlockSpec((SLOTS, hw), lambda b, s, pt_ref: (b, 0)),
                  pl.BlockSpec((SLOTS, hw), lambda b, s, pt_ref: (b, 1)),
                  pl.BlockSpec((SLOTS, hw), lambda b, s, pt_ref: (b, 2)),
                  pl.BlockSpec((rows, 1), lambda b, s, pt_ref: (0, 0)),
                  pl.BlockSpec((n_heads, hw), lambda b, s, pt_ref: (0, 0)),
                  pl.BlockSpec(uu.shape, lambda b, s, pt_ref: (0, 0))] + page_specs + page_specs,
        out_specs=pl.BlockSpec((SLOTS, hw), lambda b, s, pt_ref: (b, 0)),
        scratch_shapes=[pltpu.VMEM((rows, hw), BF16), pltpu.VMEM((rows, hw), F32),
                        pltpu.VMEM((rows, LANES), F32)],
    )
    return pl.pallas_call(
        kern,
        out_shape=jax.ShapeDtypeStruct((n, hw), F32),
        grid_spec=grid_spec,
        compiler_params=_params("parallel", "arbitrary"),
        name="sb_sample",
    )(pt, qkv, qkv, qkv, bias_col, head_mask, uu, *([pool_k] * g), *([pool_v] * g))


def _expand_heads(v, ex2):
    return _dot(_hi_lo(v, 1), ex2)


def _ssd_diag(acs, cb, xs, same_seq, half_masks, h0):
    acs_t = acs.T
    y = None
    for c in range(2):
        h = h0 + c
        seg = acs[:, h:h + 1] - acs_t[h:h + 1, :]
        m = (cb * jnp.exp(jnp.where(same_seq, seg, NEG))).astype(BF16)
        term = _dot(m, jnp.where(half_masks[c], xs, 0.0).astype(BF16))
        y = term if y is None else y + term
    return y


def _ssd_prompt_kernel(z_ref, x_ref, bc_ref, xh_ref, bch_ref, dt_ref, cwx_ref, cbx_ref, cwb_ref,
                       cbb_ref, dtb_ref, a_ref, d_ref, nw_ref, tt_ref, ex_ref, yg_ref, st_ref,
                       state_ref, *, groups, d_state, lane_chunk):
    lc, di = x_ref.shape
    c_idx = pl.program_id(1)
    not_first = c_idx > 0

    @pl.when(c_idx == 0)
    def _():
        state_ref[...] = jnp.zeros_like(state_ref)

    row = lax.broadcasted_iota(jnp.int32, (SUBLANES, lane_chunk), 0)

    def conv_silu(cur_ref, halo_ref, w_ref, b_ref):
        outs = []
        for c0 in range(0, cur_ref.shape[1], lane_chunk):
            sl = slice(c0, c0 + lane_chunk)
            halo = jnp.where(not_first, halo_ref[:, sl].astype(F32), 0.0)
            outs.append(_silu(_conv_taps(cur_ref[:, sl].astype(F32), halo, w_ref, b_ref, sl, row)))
        return jnp.concatenate(outs, axis=1)

    xc = conv_silu(x_ref, xh_ref, cwx_ref, cbx_ref)
    bcc = conv_silu(bc_ref, bch_ref, cwb_ref, cbb_ref)
    gn = groups * d_state
    dt = _softplus(dt_ref[...] + dtb_ref[...])
    a = dt * a_ref[...]
    acs = _dot(tt_ref[...], _hi_lo(a, 0))
    dt_full = _expand_heads(dt, ex_ref[...])
    acs_full = _expand_heads(acs, ex_ref[...])
    eacs_full = jnp.exp(acs_full)
    te_full = jnp.exp(_expand_heads(acs[lc - 1:lc, :] - acs, ex_ref[...]))
    cd_full = eacs_full[lc - 1:lc, :]
    xdt = xc * dt_full
    ti = lax.broadcasted_iota(jnp.int32, (lc, lc), 0)
    si = lax.broadcasted_iota(jnp.int32, (lc, lc), 1)
    causal = si <= ti
    lane = lax.broadcasted_iota(jnp.int32, (lc, LANES), 1)
    half_masks = (lane < HEAD_DIM, lane >= HEAD_DIM)
    pairs_per_group = di // LANES // groups
    ys = []
    for g in range(groups):
        bg = bcc[:, g * d_state:(g + 1) * d_state]
        cg = bcc[:, gn + g * d_state:gn + (g + 1) * d_state].astype(BF16)
        cb = _dotg(cg, bg.astype(BF16), NT)
        bg_t = bg.T.astype(BF16)
        for pp in range(pairs_per_group):
            p = g * pairs_per_group + pp
            sl = slice(p * LANES, (p + 1) * LANES)
            xs = xdt[:, sl]
            st = state_ref[p]
            y = _ssd_diag(acs, cb, xs, causal, half_masks, 2 * p)
            y = y + _dot(cg, st.astype(BF16)) * eacs_full[:, sl] + xc[:, sl] * d_ref[:, sl]
            state_ref[p] = st * cd_full[:, sl] + _dot(bg_t, (xs * te_full[:, sl]).astype(BF16))
            ys.append(y)
    y = jnp.concatenate(ys, axis=1)
    yz = y * _silu(z_ref[...].astype(F32))
    yg_ref[...] = _rms(yz, nw_ref[...]).astype(BF16)

    @pl.when(c_idx == pl.num_programs(1) - 1)
    def _():
        st_ref[0] = state_ref[...]


def _ssd_prompt(zx, dt, cwx, cbx, cwb, cbb, dtb, a, dfull, nw, tt, ex2, batch, seq_len, chunk,
                groups, d_state):
    n = zx.shape[0]
    di = dfull.shape[1]
    nc = seq_len // chunk
    hb = chunk // SUBLANES
    pairs = di // LANES
    kern = functools.partial(_ssd_prompt_kernel, groups=groups, d_state=d_state, lane_chunk=512)
    cur = lambda b, c: b * nc + c
    halo = lambda b, c: jnp.maximum((b * nc + c) * hb - 1, 0)
    full = lambda arr: pl.BlockSpec(arr.shape, lambda b, c: (0, 0))
    return pl.pallas_call(
        kern,
        out_shape=(jax.ShapeDtypeStruct((n, di), BF16),
                   jax.ShapeDtypeStruct((batch, pairs, d_state, LANES), F32)),
        grid=(batch, nc),
        in_specs=[pl.BlockSpec((chunk, di), lambda b, c: (cur(b, c), 0)),
                  pl.BlockSpec((chunk, di), lambda b, c: (cur(b, c), 1)),
                  pl.BlockSpec((chunk, di), lambda b, c: (cur(b, c), 2)),
                  pl.BlockSpec((SUBLANES, di), lambda b, c: (halo(b, c), 1)),
                  pl.BlockSpec((SUBLANES, di), lambda b, c: (halo(b, c), 2)),
                  pl.BlockSpec((chunk, LANES), lambda b, c: (cur(b, c), 0)),
                  full(cwx), full(cbx), full(cwb), full(cbb), full(dtb), full(a), full(dfull),
                  full(nw), full(tt), full(ex2)],
        out_specs=(pl.BlockSpec((chunk, di), lambda b, c: (cur(b, c), 0)),
                   pl.BlockSpec((1, pairs, d_state, LANES), lambda b, c: (b, 0, 0, 0))),
        scratch_shapes=[pltpu.VMEM((pairs, d_state, LANES), F32)],
        compiler_params=_params("parallel", "arbitrary"),
        name="ssd_prompt",
    )(zx, zx, zx, zx, zx, dt, cwx, cbx, cwb, cbb, dtb, a, dfull, nw, tt, ex2)


def _ssd_sample_kernel(z_ref, x_ref, bc_ref, hx_ref, hbc_ref, dt_ref, cwx_ref, cbx_ref, cwb_ref,
                       cbb_ref, dtb_ref, a_ref, d_ref, nw_ref, ex_ref, sel_ref, st_in_ref,
                       yg_ref, st_out_ref, c_s, b_s, xw_s, ea_s, y_s,
                       *, groups, d_state, tok0, lane_chunk):
    rows, di = x_ref.shape
    bi = pl.program_id(1)
    gn = groups * d_state
    pairs_per_group = di // LANES // groups

    @pl.when(bi == 0)
    def _():
        slot_c = lax.broadcasted_iota(jnp.int32, (rows, lane_chunk), 0) % SLOTS

        def conv_silu(cur_ref, hist_ref, w_ref, b_ref):
            outs = []
            for c0 in range(0, cur_ref.shape[1], lane_chunk):
                sl = slice(c0, c0 + lane_chunk)
                ext = jnp.where(slot_c < tok0, hist_ref[:, sl], cur_ref[:, sl].astype(F32))
                outs.append(_silu(_conv_taps_slots(ext, w_ref, b_ref, sl)))
            return jnp.concatenate(outs, axis=1)

        xc = conv_silu(x_ref, hx_ref, cwx_ref, cbx_ref)
        bcc = conv_silu(bc_ref, hbc_ref, cwb_ref, cbb_ref)
        slot = lax.broadcasted_iota(jnp.int32, (rows, LANES), 0) % SLOTS
        dt = jnp.where(slot >= tok0, _softplus(dt_ref[...] + dtb_ref[...]), 0.0)
        a = dt * a_ref[...]
        acs, rev = a, jnp.zeros_like(a)
        for d in range(1, SLOTS - tok0):
            acs = acs + jnp.where(slot - d >= tok0, pltpu.roll(a, d, 0), 0.0)
            rev = rev + jnp.where(slot + d < SLOTS, pltpu.roll(a, rows - d, 0), 0.0)
        dt_full = _expand_heads(dt, ex_ref[...])
        acs_full = _expand_heads(acs, ex_ref[...])
        te_full = jnp.exp(_expand_heads(rev, ex_ref[...]))
        xdt = xc * dt_full
        ti = lax.broadcasted_iota(jnp.int32, (rows, rows), 0)
        si = lax.broadcasted_iota(jnp.int32, (rows, rows), 1)
        same_seq = (si <= ti) & (si // SLOTS == ti // SLOTS)
        lane = lax.broadcasted_iota(jnp.int32, (rows, LANES), 1)
        half_masks = (lane < HEAD_DIM, lane >= HEAD_DIM)
        ys = []
        for g in range(groups):
            bg = bcc[:, g * d_state:(g + 1) * d_state].astype(BF16)
            cg = bcc[:, gn + g * d_state:gn + (g + 1) * d_state].astype(BF16)
            cb = _dotg(cg, bg, NT)
            for pp in range(pairs_per_group):
                p = g * pairs_per_group + pp
                sl = slice(p * LANES, (p + 1) * LANES)
                ys.append(_ssd_diag(acs, cb, xdt[:, sl], same_seq, half_masks, 2 * p)
                          + xc[:, sl] * d_ref[:, sl])
        y_s[...] = jnp.concatenate(ys, axis=1)
        b_s[...] = bcc[:, :gn]
        c_s[...] = bcc[:, gn:]
        xw_s[...] = xdt * te_full
        ea_s[...] = jnp.exp(acs_full)

    r0 = pl.multiple_of(bi * SLOTS, SLOTS)
    pad = jnp.zeros((SLOTS, LANES), F32)
    cb_rows = c_s[pl.ds(r0, SLOTS), :]
    bb_rows = b_s[pl.ds(r0, SLOTS), :]
    xw_rows = xw_s[pl.ds(r0, SLOTS), :]
    ea_rows = ea_s[pl.ds(r0, SLOTS), :]
    y_off = []
    for g in range(groups):
        gs = slice(g * d_state, (g + 1) * d_state)
        cg = jnp.concatenate([cb_rows[:, gs], pad], axis=0).astype(BF16)
        bg = jnp.concatenate([bb_rows[:, gs], pad], axis=0).astype(BF16)
        for pp in range(pairs_per_group):
            p = g * pairs_per_group + pp
            sl = slice(p * LANES, (p + 1) * LANES)
            st = st_in_ref[0, p]
            y_off.append(_dotg(cg, st.astype(BF16), NT)[:SLOTS] * ea_rows[:, sl])
            decay = _dotg(_hi_lo(ea_rows[:, sl], 0), sel_ref[...], TN)
            xw = jnp.concatenate([xw_rows[:, sl], pad], axis=0).astype(BF16)
            st_out_ref[0, p] = st * decay + _dotg(xw, bg, TN)
    y_s[pl.ds(r0, SLOTS), :] = y_s[pl.ds(r0, SLOTS), :] + jnp.concatenate(y_off, axis=1)

    @pl.when(bi == pl.num_programs(1) - 1)
    def _():
        yz = y_s[...] * _silu(z_ref[...].astype(F32))
        yg_ref[...] = _rms(yz, nw_ref[...]).astype(BF16)


def _ssd_sample(zx, hist, dt, cwx, cbx, cwb, cbb, dtb, a, dfull, nw, ex2, sel, state, groups,
                d_state, tok0, *, bb=16):
    n = zx.shape[0]
    bs = state.shape[0]
    di = dfull.shape[1]
    pairs = di // LANES
    bb = min(bb, bs)
    rows = bb * SLOTS
    gn = groups * d_state
    kern = functools.partial(_ssd_sample_kernel, groups=groups, d_state=d_state, tok0=tok0,
                             lane_chunk=512)
    full = lambda arr: pl.BlockSpec(arr.shape, lambda i, j: (0, 0))
    st_spec = pl.BlockSpec((1, pairs, LANES, d_state), lambda i, j: (i * bb + j, 0, 0, 0))
    return pl.pallas_call(
        kern,
        out_shape=(jax.ShapeDtypeStruct((n, di), BF16),
                   jax.ShapeDtypeStruct(state.shape, F32)),
        grid=(bs // bb, bb),
        in_specs=[pl.BlockSpec((rows, di), lambda i, j: (i, 0)),
                  pl.BlockSpec((rows, di), lambda i, j: (i, 1)),
                  pl.BlockSpec((rows, di), lambda i, j: (i, 2)),
                  pl.BlockSpec((rows, di), lambda i, j: (i, 0)),
                  pl.BlockSpec((rows, di), lambda i, j: (i, 1)),
                  pl.BlockSpec((rows, LANES), lambda i, j: (i, 0)),
                  full(cwx), full(cbx), full(cwb), full(cbb), full(dtb), full(a), full(dfull),
                  full(nw), full(ex2), full(sel), st_spec],
        out_specs=(pl.BlockSpec((rows, di), lambda i, j: (i, 0)), st_spec),
        scratch_shapes=[pltpu.VMEM((rows, gn), F32), pltpu.VMEM((rows, gn), F32),
                        pltpu.VMEM((rows, di), F32), pltpu.VMEM((rows, di), F32),
                        pltpu.VMEM((rows, di), F32)],
        compiler_params=_params("parallel", "arbitrary"),
        name="ssd_sample",
    )(zx, zx, zx, hist, hist, dt, cwx, cbx, cwb, cbb, dtb, a, dfull, nw, ex2, sel, state)


def _widen_heads(w, halves):
    d = w.shape[0]
    h = len(halves)
    onehot = jax.nn.one_hot(jnp.asarray(halves), 2, dtype=w.dtype)
    wide = w.reshape(d, h, 1, HEAD_DIM) * onehot[None, :, :, None]
    return wide.reshape(d, h * LANES)


def _suffix_sum_matrix(n, sign):
    j = jnp.arange(n)[:, None]
    s = jnp.arange(n)[None, :]
    u = sign * jnp.concatenate([(j >= s).astype(F32), jnp.ones((n, LANES), F32)], axis=1)
    return jnp.concatenate([u, u], axis=0).astype(BF16)


def _pad_slots(a, lead):
    k = a.shape[-2]
    pad = [(0, 0)] * (a.ndim - 2) + [(lead, SLOTS - lead - k), (0, 0)]
    a = jnp.pad(a, pad)
    return a.reshape(a.shape[:-3] + (a.shape[-3] * SLOTS, a.shape[-1]))


def kernel(x_prompt, x_sample, mem_prompt, cache_swa_k, cache_swa_v, state_ssm, state_ssm_conv, cache_sb_k, cache_sb_v, cache_mem_k, cache_mem_v, state_ffn_conv, page_table, norm_mix_w, norm_xattn_w, norm_ffn_w, final_norm_w, mem_token_norm_w, swa_wqkv, swa_sinks, swa_wo, ssm_w_in, ssm_conv_w, ssm_conv_b, ssm_dt_bias, ssm_a_log, ssm_d, ssm_norm_w, ssm_w_out, sb_wqkv, sb_logit_bias, sb_wo, mem_wq, mem_wkv, mem_wo, ffn_w_in, ffn_conv_w, ffn_conv_b, ffn_w_out):
    bp, seq, d = x_prompt.shape
    bs, n_tok, _ = x_sample.shape
    depth = norm_mix_w.shape[0]
    tok0 = SLOTS - n_tok
    scale = 1.0 / math.sqrt(HEAD_DIM)
    row = lambda v: v.reshape(1, -1).astype(F32)

    window, swa_kvh = cache_swa_k.shape[2], cache_swa_k.shape[3]
    swa_heads = swa_sinks.shape[1]
    swa_group = swa_heads // swa_kvh
    swa_kvw = swa_kvh * HEAD_DIM
    ssm_heads, ssm_p, d_state = state_ssm.shape[2], state_ssm.shape[3], state_ssm.shape[4]
    d_inner = ssm_heads * ssm_p
    conv_dim = ssm_conv_w.shape[2]
    gn = (conv_dim - d_inner) // 2
    groups = gn // d_state
    ssm_k = ssm_conv_w.shape[1]
    sb_heads = sb_logit_bias.shape[1]
    sb_w = sb_heads * HEAD_DIM
    page_size = cache_sb_k.shape[2]
    n_mem, mem_heads = cache_mem_k.shape[2], cache_mem_k.shape[3]
    mem_w = mem_heads * HEAD_DIM
    d_ff = ffn_conv_w.shape[2]
    ffn_k = ffn_conv_w.shape[1]
    assert ssm_p == HEAD_DIM and d_state == LANES and ssm_heads <= LANES
    assert n_tok + max(ssm_k, ffn_k) - 1 <= SLOTS and page_size == LANES and 2 * gn == d_inner
    chunk = math.gcd(seq, 128)

    sb_tq = next(t for t in (512, 256, 128) if seq % t == 0)
    sb_tk = min(256, sb_tq)
    kk = jnp.arange(sb_tk)
    uu_blk = -(kk[:, None] >= kk[None, :]).astype(BF16)
    uu_page = _suffix_sum_matrix(page_size, 1.0)
    sb_head_mask = (jnp.arange(sb_w)[None, :] // HEAD_DIM == jnp.arange(sb_heads)[:, None]).astype(F32)
    tri = (jnp.arange(chunk)[None, :] <= jnp.arange(chunk)[:, None]).astype(BF16)
    tt = jnp.concatenate([tri, tri], axis=1)
    ex = (jnp.arange(d_inner)[None, :] // ssm_p == jnp.arange(LANES)[:, None]).astype(BF16)
    ex2 = jnp.concatenate([ex, ex], axis=0)
    sel = jnp.zeros((2 * SLOTS, d_state), F32).at[SLOTS - 1].set(1.0).at[2 * SLOTS - 1].set(1.0)
    sel = sel.astype(BF16)

    xp = x_prompt.reshape(bp * seq, d)
    xs = _pad_slots(x_sample, tok0)

    wkv_all = jnp.transpose(mem_wkv, (1, 0, 2)).reshape(d, depth * 2 * mem_w).astype(BF16)
    mem_kv = _rms_linear(mem_prompt.reshape(bp * n_mem, d), row(mem_token_norm_w), wkv_all, F32,
                         name="mem_kv")
    mem_kv5 = mem_kv.reshape(bp, n_mem, depth, 2, mem_heads, HEAD_DIM)
    mem_k_prompt = jnp.moveaxis(mem_kv5[:, :, :, 0], 2, 0)
    mem_v_prompt = jnp.moveaxis(mem_kv5[:, :, :, 1], 2, 0)
    mem_kv_b = mem_kv.astype(BF16)

    ffn_hist = _pad_slots(state_ffn_conv, tok0 - (ffn_k - 1))
    cache_t = lambda c: jnp.transpose(c, (0, 1, 3, 4, 2)).reshape(depth * bs, mem_w, n_mem)
    mem_cache_kt, mem_cache_vt = cache_t(cache_mem_k), cache_t(cache_mem_v)

    outs = {k: [] for k in ("swa_kp", "swa_vp", "swa_ks", "swa_vs", "ssm_sp", "ssm_cp", "ssm_ss",
                            "ssm_cs", "sb_kp", "sb_vp", "sb_ks", "sb_vs", "ffn_cp", "ffn_cs")}

    for i in range(depth):
        kind, j = i % 3, i // 3
        nw_mix = row(norm_mix_w[i])
        if kind == 0:
            halves = [(h // swa_group) % 2 for h in range(swa_heads)]
            qw = swa_heads * HEAD_DIM
            w_all = jnp.concatenate([_widen_heads(swa_wqkv[j][:, :qw] * scale, halves),
                                     swa_wqkv[j][:, qw:]], axis=1).astype(BF16)
            order = [(2 * p + c) * swa_group + r for p in range(swa_kvh // 2)
                     for r in range(swa_group) for c in range(2)]
            wo = swa_wo[j].reshape(swa_heads, HEAD_DIM, d)[jnp.asarray(order)].reshape(qw, d).astype(BF16)
            slopes = jnp.exp2(-8.0 * jnp.arange(1, swa_heads + 1, dtype=F32) / swa_heads)
            sm = jnp.stack([slopes, swa_sinks[j].astype(F32)])
            kcol = swa_heads * LANES

            qkv_p = _rms_linear(xp, nw_mix, w_all, BF16, name="swa_qkv_prompt")
            o_p = _swa_prompt(qkv_p, sm, bp, seq, window, swa_heads, swa_group)
            xp = _linear_res(o_p, wo, xp, name="swa_out_prompt")
            kv_p = qkv_p.reshape(bp, seq, -1)[:, seq - window:, kcol:].astype(F32)
            outs["swa_kp"].append(kv_p[..., :swa_kvw].reshape(bp, window, swa_kvh, HEAD_DIM))
            outs["swa_vp"].append(kv_p[..., swa_kvw:].reshape(bp, window, swa_kvh, HEAD_DIM))

            qkv_s = _rms_linear(xs, nw_mix, w_all, BF16, name="swa_qkv_sample")
            o_s = _swa_sample(qkv_s, sm, cache_swa_k[j].reshape(bs, window, swa_kvw),
                              cache_swa_v[j].reshape(bs, window, swa_kvw), swa_heads, swa_group, tok0)
            xs = _linear_res(o_s, wo, xs, name="swa_out_sample")
            kv_s = qkv_s.reshape(bs, SLOTS, -1)[:, tok0:, kcol:].astype(F32)
            k_new = kv_s[..., :swa_kvw].reshape(bs, n_tok, swa_kvh, HEAD_DIM)
            v_new = kv_s[..., swa_kvw:].reshape(bs, n_tok, swa_kvh, HEAD_DIM)
            outs["swa_ks"].append(jnp.concatenate([cache_swa_k[j][:, n_tok:], k_new], axis=1))
            outs["swa_vs"].append(jnp.concatenate([cache_swa_v[j][:, n_tok:], v_new], axis=1))
        elif kind == 1:
            zxw = 2 * d_inner + 2 * gn
            w_main = ssm_w_in[j][:, :zxw].astype(BF16)
            w_dt = jnp.pad(ssm_w_in[j][:, zxw:], ((0, 0), (0, LANES - ssm_heads))).astype(BF16)
            pad_h = lambda v: jnp.pad(v.astype(F32), (0, LANES - ssm_heads)).reshape(1, LANES)
            cwx, cwb = ssm_conv_w[j][:, :d_inner], ssm_conv_w[j][:, d_inner:]
            cbx, cbb = row(ssm_conv_b[j][:d_inner]), row(ssm_conv_b[j][d_inner:])
            dtb = pad_h(ssm_dt_bias[j])
            a_neg = pad_h(-jnp.exp(ssm_a_log[j].astype(F32)))
            dfull = row(jnp.repeat(ssm_d[j].astype(F32), ssm_p))
            nw_ssm = row(ssm_norm_w[j])
            w_out = ssm_w_out[j].astype(BF16)

            zx_p = _rms_linear(xp, nw_mix, w_main, BF16, name="ssm_in_prompt")
            dt_p = _rms_linear(xp, nw_mix, w_dt, F32, name="ssm_dt_prompt")
            yg_p, st_p = _ssd_prompt(zx_p, dt_p, cwx, cbx, cwb, cbb, dtb, a_neg, dfull, nw_ssm, tt,
                                     ex2, bp, seq, chunk, groups, d_state)
            xp = _linear_res(yg_p, w_out, xp, name="ssm_out_prompt")
            outs["ssm_sp"].append(jnp.swapaxes(st_p, 2, 3).reshape(bp, ssm_heads, ssm_p, d_state))
            outs["ssm_cp"].append(
                zx_p.reshape(bp, seq, -1)[:, seq - (ssm_k - 1):, d_inner:].astype(F32))

            zx_s = _rms_linear(xs, nw_mix, w_main, BF16, name="ssm_in_sample")
            dt_s = _rms_linear(xs, nw_mix, w_dt, F32, name="ssm_dt_sample")
            hist = _pad_slots(state_ssm_conv[j], tok0 - (ssm_k - 1))
            yg_s, st_s = _ssd_sample(zx_s, hist, dt_s, cwx, cbx, cwb, cbb, dtb, a_neg, dfull, nw_ssm,
                                     ex2, sel, state_ssm[j].reshape(bs, d_inner // LANES, LANES, d_state),
                                     groups, d_state, tok0)
            xs = _linear_res(yg_s, w_out, xs, name="ssm_out_sample")
            outs["ssm_ss"].append(st_s.reshape(bs, ssm_heads, ssm_p, d_state))
            ext = jnp.concatenate(
                [state_ssm_conv[j], zx_s.reshape(bs, SLOTS, -1)[:, tok0:, d_inner:].astype(F32)], axis=1)
            outs["ssm_cs"].append(ext[:, -(ssm_k - 1):])
        else:
            wq, wk, wv = (sb_wqkv[j][:, t * sb_w:(t + 1) * sb_w] for t in range(3))
            w_p = jnp.concatenate([_widen_heads(wq * (scale * LOG2E), [h % 2 for h in range(sb_heads)]),
                                   wk, wv], axis=1).astype(BF16)
            w_s = jnp.concatenate([wq * scale, wk, wv], axis=1).astype(BF16)
            bias = sb_logit_bias[j].astype(F32)
            wo = sb_wo[j].astype(BF16)
            kcol = sb_heads * LANES

            wkv_t = jnp.concatenate([wk, wv], axis=1).T.astype(BF16)
            qkv_p, k_t, v_t = _rms_linear_t(xp, nw_mix, w_p, wkv_t, 2, seq, name="sb_qkv_prompt")
            o_p = _sb_prompt(qkv_p, bias * LOG2E, uu_blk, bp, seq, sb_heads, sb_tq, sb_tk)
            xp = _linear_res(o_p, wo, xp, name="sb_out_prompt")
            heads_last = lambda t: jnp.transpose(t.reshape(bp, sb_heads, HEAD_DIM, seq), (0, 3, 1, 2))
            outs["sb_kp"].append(heads_last(k_t))
            outs["sb_vp"].append(heads_last(v_t))

            qkv_s = _rms_linear(xs, nw_mix, w_s, F32, name="sb_qkv_sample")
            n_c, n_pool = cache_sb_k.shape[:2]
            pages_t = lambda c: jnp.transpose(c, (0, 1, 3, 4, 2)).reshape(n_c * n_pool, sb_w, page_size)
            o_s = _sb_sample(qkv_s, pages_t(cache_sb_k), pages_t(cache_sb_v), j * n_pool, page_table,
                             jnp.tile(bias, n_tok).reshape(-1, 1), sb_head_mask, uu_page,
                             sb_heads, n_tok, tok0)
            xs = _linear_res(o_s, wo, xs, name="sb_out_sample")
            kv_s = qkv_s.reshape(bs, SLOTS, -1)[:, tok0:, sb_w:]
            outs["sb_ks"].append(kv_s[..., :sb_w].reshape(bs, n_tok, sb_heads, HEAD_DIM))
            outs["sb_vs"].append(kv_s[..., sb_w:].reshape(bs, n_tok, sb_heads, HEAD_DIM))

        nw_x = row(norm_xattn_w[i])
        wq_m = _widen_heads(mem_wq[i] * scale, [h % 2 for h in range(mem_heads)]).astype(BF16)
        wo_m = mem_wo[i].astype(BF16)
        k_m = mem_kv_b[:, (2 * i) * mem_w:(2 * i + 1) * mem_w]
        v_m = mem_kv_b[:, (2 * i + 1) * mem_w:(2 * i + 2) * mem_w]
        xp = _mem_attn_prompt(xp, nw_x, wq_m, k_m, v_m, wo_m, seq, n_mem)
        xs = _mem_attn_sample(xs, nw_x, wq_m, mem_cache_kt, mem_cache_vt, wo_m, i)

        nw_f = row(norm_ffn_w[i])
        w_in = ffn_w_in[i].astype(BF16)
        w_out = ffn_w_out[i].astype(BF16)
        cw, cb = ffn_conv_w[i].astype(F32), row(ffn_conv_b[i])
        gu_p = _rms_linear(xp, nw_f, w_in, BF16, name="ffn_in_prompt")
        xp = _glu_prompt(gu_p, cw, cb, w_out, xp, seq)
        outs["ffn_cp"].append(gu_p.reshape(bp, seq, -1)[:, seq - (ffn_k - 1):, :d_ff].astype(F32))
        gu_s = _rms_linear(xs, nw_f, w_in, BF16, name="ffn_in_sample")
        xs = _glu_sample(gu_s, ffn_hist[i], cw, cb, w_out, xs, tok0)
        ext = jnp.concatenate(
            [state_ffn_conv[i], gu_s.reshape(bs, SLOTS, -1)[:, tok0:, :d_ff].astype(F32)], axis=1)
        outs["ffn_cs"].append(ext[:, -(ffn_k - 1):])

    y_prompt = _rmsnorm(xp, row(final_norm_w)).reshape(bp, seq, d)
    y_sample = _rmsnorm(xs, row(final_norm_w)).reshape(bs, SLOTS, d)[:, tok0:]
    st = lambda k: jnp.stack(outs[k])
    return (y_prompt, y_sample, st("swa_kp"), st("swa_vp"), st("swa_ks"), st("swa_vs"),
            st("ssm_sp"), st("ssm_cp"), st("ssm_ss"), st("ssm_cs"),
            st("sb_kp"), st("sb_vp"), st("sb_ks"), st("sb_vs"),
            mem_k_prompt, mem_v_prompt, st("ffn_cp"), st("ffn_cs"))
```

```python
import functools
import math

import jax
import jax.numpy as jnp
from jax import lax
from jax.experimental import pallas as pl
from jax.experimental.pallas import tpu as pltpu

F32 = jnp.float32
BF16 = jnp.bfloat16
EPS = 1e-6
HEAD_DIM = 64
LANES = 128
SUBLANES = 8
SLOTS = SUBLANES
NEG = -1e30
VMEM_LIMIT = 56 * 1024 * 1024
LOG2E = 1.0 / math.log(2.0)
NT = (((1,), (1,)), ((), ()))
TN = (((0,), (0,)), ((), ()))


def _params(*sem):
    return pltpu.CompilerParams(dimension_semantics=sem, vmem_limit_bytes=VMEM_LIMIT)


def _tile(n, want):
    if n <= want:
        return n
    t = (want // LANES) * LANES
    while t > LANES and n % t:
        t -= LANES
    assert n % t == 0, (n, want)
    return t


def _dot(a, b):
    return jnp.dot(a, b, preferred_element_type=F32)


def _dotg(a, b, dims):
    return lax.dot_general(a, b, dims, preferred_element_type=F32)


def _hi_lo(v, axis):
    hi = v.astype(BF16).astype(F32)
    return jnp.concatenate([hi, v - hi], axis=axis).astype(BF16)


def _softplus(z):
    return jnp.maximum(z, 0.0) + jnp.log(1.0 + jnp.exp(-jnp.abs(z)))


def _silu(v):
    return v * jax.nn.sigmoid(v)


def _rms(x, w):
    ms = jnp.mean(x * x, axis=-1, keepdims=True)
    return x * lax.rsqrt(ms + EPS) * w


def _rms_linear_kernel(x_ref, nw_ref, w_ref, *rest, tn, with_f32):
    if with_f32:
        w2_ref, o_ref, o2_ref, xn_ref = rest
    else:
        o_ref, xn_ref = rest
    xn_ref[...] = _rms(x_ref[...], nw_ref[...]).astype(BF16)
    for c0 in range(0, w_ref.shape[1], tn):
        o_ref[:, c0:c0 + tn] = _dot(xn_ref[...], w_ref[:, c0:c0 + tn]).astype(o_ref.dtype)
    if with_f32:
        o2_ref[...] = _dot(xn_ref[...], w2_ref[...])


def _rms_linear(x, nw, w, out_dtype, w_f32=None, *, tm=512, tn=512, name="rms_linear"):
    n, d = x.shape
    f = w.shape[1]
    tm, tn = _tile(n, tm), _tile(f, tn)
    with_f32 = w_f32 is not None
    in_specs = [pl.BlockSpec((tm, d), lambda i: (i, 0)),
                pl.BlockSpec((1, d), lambda i: (0, 0)),
                pl.BlockSpec((d, f), lambda i: (0, 0), pipeline_mode=pl.Buffered(1))]
    out_shape = jax.ShapeDtypeStruct((n, f), out_dtype)
    out_specs = pl.BlockSpec((tm, f), lambda i: (i, 0))
    args = (x, nw, w)
    if with_f32:
        f2 = w_f32.shape[1]
        in_specs.append(pl.BlockSpec((d, f2), lambda i: (0, 0)))
        out_shape = (out_shape, jax.ShapeDtypeStruct((n, f2), F32))
        out_specs = (out_specs, pl.BlockSpec((tm, f2), lambda i: (i, 0)))
        args += (w_f32,)
    return pl.pallas_call(
        functools.partial(_rms_linear_kernel, tn=tn, with_f32=with_f32),
        out_shape=out_shape,
        grid=(n // tm,),
        in_specs=in_specs,
        out_specs=out_specs,
        scratch_shapes=[pltpu.VMEM((tm, d), BF16)],
        compiler_params=_params("parallel"),
        name=name,
    )(*args)


def _rms_linear_t_kernel(x_ref, nw_ref, w_ref, wt_ref, o_ref, *rest, tn, n_t):
    t_refs, xn_ref = rest[:n_t], rest[n_t]
    xn_ref[...] = _rms(x_ref[...], nw_ref[...]).astype(BF16)
    for c0 in range(0, w_ref.shape[1], tn):
        o_ref[:, c0:c0 + tn] = _dot(xn_ref[...], w_ref[:, c0:c0 + tn]).astype(o_ref.dtype)
    rows = t_refs[0].shape[0]
    for t, t_ref in enumerate(t_refs):
        for r0 in range(0, rows, 256):
            wt = wt_ref[t * rows + r0:t * rows + r0 + 256, :]
            t_ref[r0:r0 + 256, :] = _dotg(wt, xn_ref[...], NT)


def _rms_linear_t(x, nw, w, wt, n_t, seq_len, *, tm=512, tn=512, name="rms_linear_t"):
    n, d = x.shape
    f = w.shape[1]
    rows = wt.shape[0] // n_t
    tm, tn = _tile(seq_len, tm), _tile(f, tn)
    tps = seq_len // tm
    t_shape = jax.ShapeDtypeStruct((n // seq_len, rows, seq_len), F32)
    t_spec = pl.BlockSpec((None, rows, tm), lambda i: (i // tps, 0, i % tps))
    return pl.pallas_call(
        functools.partial(_rms_linear_t_kernel, tn=tn, n_t=n_t),
        out_shape=(jax.ShapeDtypeStruct((n, f), BF16),) + (t_shape,) * n_t,
        grid=(n // tm,),
        in_specs=[pl.BlockSpec((tm, d), lambda i: (i, 0)),
                  pl.BlockSpec((1, d), lambda i: (0, 0)),
                  pl.BlockSpec((d, f), lambda i: (0, 0), pipeline_mode=pl.Buffered(1)),
                  pl.BlockSpec(wt.shape, lambda i: (0, 0), pipeline_mode=pl.Buffered(1))],
        out_specs=(pl.BlockSpec((tm, f), lambda i: (i, 0)),) + (t_spec,) * n_t,
        scratch_shapes=[pltpu.VMEM((tm, d), BF16)],
        compiler_params=_params("parallel"),
        name=name,
    )(x, nw, w, wt)


def _linear_res_kernel(y_ref, w_ref, r_ref, o_ref):
    o_ref[...] = r_ref[...] + _dot(y_ref[...].astype(BF16), w_ref[...])


def _linear_res(y, w, res, *, tm=512, name="linear_res"):
    n, k = y.shape
    d = w.shape[1]
    tm = _tile(n, tm)
    return pl.pallas_call(
        _linear_res_kernel,
        out_shape=jax.ShapeDtypeStruct((n, d), F32),
        grid=(n // tm,),
        in_specs=[pl.BlockSpec((tm, k), lambda i: (i, 0)),
                  pl.BlockSpec((k, d), lambda i: (0, 0)),
                  pl.BlockSpec((tm, d), lambda i: (i, 0))],
        out_specs=pl.BlockSpec((tm, d), lambda i: (i, 0)),
        compiler_params=_params("parallel"),
        name=name,
    )(y, w, res)


def _rmsnorm_kernel(x_ref, nw_ref, o_ref):
    o_ref[...] = _rms(x_ref[...], nw_ref[...])


def _rmsnorm(x, nw, *, tm=512):
    n, d = x.shape
    tm = _tile(n, tm)
    return pl.pallas_call(
        _rmsnorm_kernel,
        out_shape=jax.ShapeDtypeStruct((n, d), F32),
        grid=(n // tm,),
        in_specs=[pl.BlockSpec((tm, d), lambda i: (i, 0)),
                  pl.BlockSpec((1, d), lambda i: (0, 0))],
        out_specs=pl.BlockSpec((tm, d), lambda i: (i, 0)),
        compiler_params=_params("parallel"),
        name="final_rmsnorm",
    )(x, nw)


def _shift_rows(cur, halo, d, row):
    if d == 0:
        return cur
    out = pltpu.roll(cur, d, 0)
    top = jnp.where(row < d, pltpu.roll(halo, d, 0), out[:SUBLANES])
    return jnp.concatenate([top, out[SUBLANES:]], axis=0)


def _conv_taps(cur, halo, w_ref, b_ref, sl, row):
    k = w_ref.shape[0]
    acc = b_ref[:, sl] + w_ref[k - 1:k, sl] * cur
    for d in range(1, k):
        acc = acc + w_ref[k - 1 - d:k - d, sl] * _shift_rows(cur, halo, d, row)
    return acc


def _conv_taps_slots(ext, w_ref, b_ref, sl):
    k = w_ref.shape[0]
    acc = b_ref[:, sl] + w_ref[k - 1:k, sl] * ext
    for d in range(1, k):
        acc = acc + w_ref[k - 1 - d:k - d, sl] * pltpu.roll(ext, d, 0)
    return acc


def _glu_prompt_kernel(g_ref, u_ref, halo_ref, cw_ref, cb_ref, w_ref, r_ref, fw_ref, o_ref, h_ref,
                       *, tiles_per_seq, lane_chunk, final_norm):
    tm, f = g_ref.shape
    not_first = (pl.program_id(0) % tiles_per_seq) != 0
    row = lax.broadcasted_iota(jnp.int32, (SUBLANES, lane_chunk), 0)
    for c0 in range(0, f, lane_chunk):
        sl = slice(c0, c0 + lane_chunk)
        g = g_ref[:, sl].astype(F32)
        halo = jnp.where(not_first, halo_ref[:, sl].astype(F32), 0.0)
        conv = _conv_taps(g, halo, cw_ref, cb_ref, sl, row)
        h_ref[:, sl] = (_silu(conv) * u_ref[:, sl].astype(F32)).astype(BF16)
    out = r_ref[...] + _dot(h_ref[...], w_ref[...])
    o_ref[...] = _rms(out, fw_ref[...]) if final_norm else out


def _glu_prompt(gu, cw, cb, w_out, res, seq_len, final_nw=None, *, tm=512):
    n, f2 = gu.shape
    f = f2 // 2
    d = w_out.shape[1]
    tm = _tile(seq_len, tm)
    hb = tm // SUBLANES
    lane_chunk = 256 if f % 256 == 0 else LANES
    kern = functools.partial(_glu_prompt_kernel, tiles_per_seq=seq_len // tm, lane_chunk=lane_chunk,
                             final_norm=final_nw is not None)
    fw = jnp.ones((1, d), F32) if final_nw is None else final_nw
    return pl.pallas_call(
        kern,
        out_shape=jax.ShapeDtypeStruct((n, d), F32),
        grid=(n // tm,),
        in_specs=[pl.BlockSpec((tm, f), lambda i: (i, 0)),
                  pl.BlockSpec((tm, f), lambda i: (i, 1)),
                  pl.BlockSpec((SUBLANES, f), lambda i: (jnp.maximum(i * hb - 1, 0), 0)),
                  pl.BlockSpec(cw.shape, lambda i: (0, 0)),
                  pl.BlockSpec((1, f), lambda i: (0, 0)),
                  pl.BlockSpec((f, d), lambda i: (0, 0)),
                  pl.BlockSpec((tm, d), lambda i: (i, 0)),
                  pl.BlockSpec((1, d), lambda i: (0, 0))],
        out_specs=pl.BlockSpec((tm, d), lambda i: (i, 0)),
        scratch_shapes=[pltpu.VMEM((tm, f), BF16)],
        compiler_params=_params("parallel"),
        name="glu_prompt",
    )(gu, gu, gu, cw, cb, w_out, res, fw)


def _glu_sample_kernel(g_ref, u_ref, hist_ref, cw_ref, cb_ref, w_ref, r_ref, o_ref, h_ref,
                       *, tok0, lane_chunk):
    tm, f = g_ref.shape
    row = lax.broadcasted_iota(jnp.int32, (tm, lane_chunk), 0)
    is_hist = (row % SLOTS) < tok0
    for c0 in range(0, f, lane_chunk):
        sl = slice(c0, c0 + lane_chunk)
        ext = jnp.where(is_hist, hist_ref[:, sl], g_ref[:, sl].astype(F32))
        conv = _conv_taps_slots(ext, cw_ref, cb_ref, sl)
        h_ref[:, sl] = (_silu(conv) * u_ref[:, sl].astype(F32)).astype(BF16)
    o_ref[...] = r_ref[...] + _dot(h_ref[...], w_ref[...])


def _glu_sample(gu, hist, cw, cb, w_out, res, tok0, *, tm=256):
    n, f2 = gu.shape
    f = f2 // 2
    d = w_out.shape[1]
    tm = _tile(n, tm)
    lane_chunk = 256 if f % 256 == 0 else LANES
    kern = functools.partial(_glu_sample_kernel, tok0=tok0, lane_chunk=lane_chunk)
    return pl.pallas_call(
        kern,
        out_shape=jax.ShapeDtypeStruct((n, d), F32),
        grid=(n // tm,),
        in_specs=[pl.BlockSpec((tm, f), lambda i: (i, 0)),
                  pl.BlockSpec((tm, f), lambda i: (i, 1)),
                  pl.BlockSpec((tm, f), lambda i: (i, 0)),
                  pl.BlockSpec(cw.shape, lambda i: (0, 0)),
                  pl.BlockSpec((1, f), lambda i: (0, 0)),
                  pl.BlockSpec((f, d), lambda i: (0, 0)),
                  pl.BlockSpec((tm, d), lambda i: (i, 0))],
        out_specs=pl.BlockSpec((tm, d), lambda i: (i, 0)),
        scratch_shapes=[pltpu.VMEM((tm, f), BF16)],
        compiler_params=_params("parallel"),
        name="glu_sample",
    )(gu, gu, hist, cw, cb, w_out, res)


def _softmax_pv(s, v, transposed):
    m = jnp.max(s, axis=-1, keepdims=True)
    p = jnp.exp(s - m)
    l = jnp.sum(p, axis=-1, keepdims=True)
    pv = _dotg(p.astype(BF16), v, NT) if transposed else _dot(p.astype(BF16), v)
    return pv / l


def _mem_heads(q, k, v, n_heads, transposed=False):
    half0 = lax.broadcasted_iota(jnp.int32, (q.shape[0], LANES), 1) < HEAD_DIM
    slabs = []
    for s in range(n_heads // 2):
        sl = slice(s * LANES, (s + 1) * LANES)
        ks, vs = (k[sl, :], v[sl, :]) if transposed else (k[:, sl], v[:, sl])
        o = []
        for c in range(2):
            qc = q[:, (2 * s + c) * LANES:(2 * s + c + 1) * LANES]
            scores = _dot(qc, ks) if transposed else _dotg(qc, ks, NT)
            o.append(_softmax_pv(scores, vs, transposed))
        slabs.append(jnp.where(half0, o[0], o[1]))
    return jnp.concatenate(slabs, axis=1)


def _mem_attn_prompt_kernel(x_ref, nw_ref, wq_ref, k_ref, v_ref, wo_ref, o_ref, *, n_heads):
    x = x_ref[...]
    q = _dot(_rms(x, nw_ref[...]).astype(BF16), wq_ref[...]).astype(BF16)
    o = _mem_heads(q, k_ref[...], v_ref[...], n_heads)
    o_ref[...] = x + _dot(o.astype(BF16), wo_ref[...])


def _mem_attn_prompt(x, nw, wq, k, v, wo, seq_len, n_mem, *, tm=512):
    n, d = x.shape
    hw = k.shape[1]
    tm = _tile(seq_len, tm)
    tps = seq_len // tm
    kern = functools.partial(_mem_attn_prompt_kernel, n_heads=hw // HEAD_DIM)
    return pl.pallas_call(
        kern,
        out_shape=jax.ShapeDtypeStruct((n, d), F32),
        grid=(n // tm,),
        in_specs=[pl.BlockSpec((tm, d), lambda i: (i, 0)),
                  pl.BlockSpec((1, d), lambda i: (0, 0)),
                  pl.BlockSpec(wq.shape, lambda i: (0, 0)),
                  pl.BlockSpec((n_mem, hw), lambda i: (i // tps, 0)),
                  pl.BlockSpec((n_mem, hw), lambda i: (i // tps, 0)),
                  pl.BlockSpec(wo.shape, lambda i: (0, 0))],
        out_specs=pl.BlockSpec((tm, d), lambda i: (i, 0)),
        compiler_params=_params("parallel"),
        name="mem_attn_prompt",
    )(x, nw, wq, k, v, wo)


def _mem_attn_sample_kernel(x_ref, nw_ref, wq_ref, k_ref, v_ref, wo_ref, o_ref, *, n_heads, bb):
    x = x_ref[...]
    q = _dot(_rms(x, nw_ref[...]).astype(BF16), wq_ref[...])
    outs = []
    for b in range(bb):
        qb = q[b * SLOTS:(b + 1) * SLOTS].astype(BF16)
        outs.append(_mem_heads(qb, k_ref[b].astype(BF16), v_ref[b].astype(BF16), n_heads,
                               transposed=True))
    o = jnp.concatenate(outs, axis=0)
    o_ref[...] = x + _dot(o.astype(BF16), wo_ref[...])


def _mem_attn_sample(x, nw, wq, k, v, wo, layer, *, bb=8):
    n, d = x.shape
    bs = n // SLOTS
    _, hw, n_mem = k.shape
    bb = min(bb, bs)
    base = layer * (bs // bb)
    kern = functools.partial(_mem_attn_sample_kernel, n_heads=hw // HEAD_DIM, bb=bb)
    return pl.pallas_call(
        kern,
        out_shape=jax.ShapeDtypeStruct((n, d), F32),
        grid=(bs // bb,),
        in_specs=[pl.BlockSpec((bb * SLOTS, d), lambda i: (i, 0)),
                  pl.BlockSpec((1, d), lambda i: (0, 0)),
                  pl.BlockSpec(wq.shape, lambda i: (0, 0)),
                  pl.BlockSpec((bb, hw, n_mem), lambda i: (base + i, 0, 0)),
                  pl.BlockSpec((bb, hw, n_mem), lambda i: (base + i, 0, 0)),
                  pl.BlockSpec(wo.shape, lambda i: (0, 0))],
        out_specs=pl.BlockSpec((bb * SLOTS, d), lambda i: (i, 0)),
        compiler_params=_params("parallel"),
        name="mem_attn_sample",
    )(x, nw, wq, k, v, wo)


def _sink_softmax_pv(s, sink, v):
    m = jnp.maximum(jnp.max(s, axis=-1, keepdims=True), sink)
    p = jnp.exp(s - m)
    l = jnp.sum(p, axis=-1, keepdims=True) + jnp.exp(sink - m)
    return _dot(p.astype(BF16), v) / l


def _swa_prompt_kernel(sm_ref, q_ref, kc_ref, vc_ref, kp_ref, vp_ref, o_ref, *, n_heads, group):
    w = q_ref.shape[0]
    n = pl.program_id(1)
    a = lax.broadcasted_iota(jnp.int32, (w, 2 * w), 0)
    j = lax.broadcasted_iota(jnp.int32, (w, 2 * w), 1)
    dist = (a + w - j).astype(F32)
    has_prev = jnp.where(n > 0, 0, w)
    bias_mask = jnp.where((j >= a) & (j <= a + w) & (j >= has_prev), 0.0, NEG)
    half0 = lax.broadcasted_iota(jnp.int32, (w, LANES), 1) < HEAD_DIM
    kv_pairs = n_heads // group // 2
    for p in range(kv_pairs):
        sl = slice(p * LANES, (p + 1) * LANES)
        kpair = jnp.concatenate([kp_ref[:, sl], kc_ref[:, sl]], axis=0)
        vpair = jnp.concatenate([vp_ref[:, sl], vc_ref[:, sl]], axis=0)
        for r in range(group):
            o = []
            for c in range(2):
                h = (2 * p + c) * group + r
                s = _dotg(q_ref[:, h * LANES:(h + 1) * LANES], kpair, NT)
                s = s - sm_ref[0, h] * dist + bias_mask
                o.append(_sink_softmax_pv(s, sm_ref[1, h], vpair))
            slab = p * group + r
            o_ref[:, slab * LANES:(slab + 1) * LANES] = jnp.where(half0, o[0], o[1]).astype(BF16)


def _swa_prompt(qkv, sm, batch, seq_len, window, n_heads, group):
    n = qkv.shape[0]
    nb = seq_len // window
    qw = n_heads * LANES
    kvw = n_heads // group * HEAD_DIM
    kcol, vcol = qw // kvw, qw // kvw + 1
    kern = functools.partial(_swa_prompt_kernel, n_heads=n_heads, group=group)
    cur = lambda b, i: b * nb + i
    prev = lambda b, i: b * nb + jnp.maximum(i - 1, 0)
    return pl.pallas_call(
        kern,
        out_shape=jax.ShapeDtypeStruct((n, n_heads * HEAD_DIM), BF16),
        grid=(batch, nb),
        in_specs=[pl.BlockSpec(memory_space=pltpu.SMEM),
                  pl.BlockSpec((window, qw), lambda b, i: (cur(b, i), 0)),
                  pl.BlockSpec((window, kvw), lambda b, i: (cur(b, i), kcol)),
                  pl.BlockSpec((window, kvw), lambda b, i: (cur(b, i), vcol)),
                  pl.BlockSpec((window, kvw), lambda b, i: (prev(b, i), kcol)),
                  pl.BlockSpec((window, kvw), lambda b, i: (prev(b, i), vcol))],
        out_specs=pl.BlockSpec((window, n_heads * HEAD_DIM), lambda b, i: (cur(b, i), 0)),
        compiler_params=_params("parallel", "parallel"),
        name="swa_prompt",
    )(sm, qkv, qkv, qkv, qkv, qkv)


def _swa_sample_kernel(sm_ref, q_ref, kn_ref, vn_ref, kc_ref, vc_ref, o_ref,
                       *, n_heads, group, bb, tok0):
    w = kc_ref.shape[1]
    rows = group * SLOTS
    row = lax.broadcasted_iota(jnp.int32, (rows, 2 * w), 0)
    j = lax.broadcasted_iota(jnp.int32, (rows, 2 * w), 1)
    tq = row % SLOTS - tok0
    in_cache = j < w
    dist_i = jnp.where(in_cache, w + tq - j, tq + tok0 + w - j)
    mask_cache = jnp.where(j >= tq, 0.0, NEG)
    mask_new = jnp.where((j >= w + tok0) & (j < w + SLOTS) & (dist_i >= 0), 0.0, NEG)
    bias_mask = jnp.where(in_cache, mask_cache, mask_new)
    dist = dist_i.astype(F32)
    rgrp = lax.broadcasted_iota(jnp.int32, (rows, 1), 0) // SLOTS
    half0 = lax.broadcasted_iota(jnp.int32, (SLOTS, LANES), 1) < HEAD_DIM
    zeros = jnp.zeros((w - SLOTS, LANES), F32)
    q_all = q_ref[...].astype(F32)
    kn_all = kn_ref[...].astype(F32)
    vn_all = vn_ref[...].astype(F32)
    kv_pairs = n_heads // group // 2
    out_rows = []
    for b in range(bb):
        rs = slice(b * SLOTS, (b + 1) * SLOTS)
        slabs = [None] * (n_heads // 2)
        for p in range(kv_pairs):
            sl = slice(p * LANES, (p + 1) * LANES)
            kfull = jnp.concatenate([kc_ref[b, :, sl], kn_all[rs, sl], zeros], axis=0).astype(BF16)
            vfull = jnp.concatenate([vc_ref[b, :, sl], vn_all[rs, sl], zeros], axis=0).astype(BF16)
            o = []
            for c in range(2):
                heads = [(2 * p + c) * group + r for r in range(group)]
                qs = jnp.concatenate([q_all[rs, h * LANES:(h + 1) * LANES] for h in heads], axis=0)
                slope = jnp.zeros((rows, 1), F32)
                sink = jnp.zeros((rows, 1), F32)
                for r, h in enumerate(heads):
                    slope = jnp.where(rgrp == r, sm_ref[0, h], slope)
                    sink = jnp.where(rgrp == r, sm_ref[1, h], sink)
                s = _dotg(qs.astype(BF16), kfull, NT) - slope * dist + bias_mask
                o.append(_sink_softmax_pv(s, sink, vfull))
            for r in range(group):
                piece = slice(r * SLOTS, (r + 1) * SLOTS)
                slabs[p * group + r] = jnp.where(half0, o[0][piece], o[1][piece])
        out_rows.append(jnp.concatenate(slabs, axis=1))
    o_ref[...] = jnp.concatenate(out_rows, axis=0).astype(BF16)


def _swa_sample(qkv, sm, kc, vc, n_heads, group, tok0, *, bb=8):
    n = qkv.shape[0]
    bs, window, kvw = kc.shape
    bb = min(bb, bs)
    qw = n_heads * LANES
    kcol, vcol = qw // kvw, qw // kvw + 1
    kern = functools.partial(_swa_sample_kernel, n_heads=n_heads, group=group, bb=bb, tok0=tok0)
    return pl.pallas_call(
        kern,
        out_shape=jax.ShapeDtypeStruct((n, n_heads * HEAD_DIM), BF16),
        grid=(bs // bb,),
        in_specs=[pl.BlockSpec(memory_space=pltpu.SMEM),
                  pl.BlockSpec((bb * SLOTS, qw), lambda i: (i, 0)),
                  pl.BlockSpec((bb * SLOTS, kvw), lambda i: (i, kcol)),
                  pl.BlockSpec((bb * SLOTS, kvw), lambda i: (i, vcol)),
                  pl.BlockSpec((bb, window, kvw), lambda i: (i, 0, 0)),
                  pl.BlockSpec((bb, window, kvw), lambda i: (i, 0, 0))],
        out_specs=pl.BlockSpec((bb * SLOTS, n_heads * HEAD_DIM), lambda i: (i, 0)),
        compiler_params=_params("parallel"),
        name="swa_sample",
    )(sm, qkv, qkv, qkv, kc, vc)


def _sb_prompt_kernel(bias_ref, q_ref, k_ref, v_ref, uu_ref, o_ref, qs_ref, brow_ref, acc_ref,
                      r_ref, *, tk):
    tq = q_ref.shape[0]
    m = 2 * tq
    pair = pl.program_id(1)
    i = pl.program_id(2)
    for c in range(2):
        qs_ref[c * tq:(c + 1) * tq, :] = q_ref[:, c * LANES:(c + 1) * LANES]
        brow_ref[c * tq:(c + 1) * tq, :] = jnp.full((tq, tk), bias_ref[2 * pair + c], F32)
    acc_ref[...] = jnp.zeros_like(acc_ref)
    r_ref[...] = jnp.zeros_like(r_ref)
    qpos = lax.broadcasted_iota(jnp.int32, (m, tk), 0) % tq + i * tq
    kcol = lax.broadcasted_iota(jnp.int32, (m, tk), 1)

    def visit(j, masked):
        start = pl.multiple_of(j * tk, tk)
        kj = k_ref[pl.ds(start, tk), :]
        vj = v_ref[pl.ds(start, tk), :]
        z = _dotg(qs_ref[...], kj, NT) + brow_ref[...]
        sp = jnp.maximum(z, 0.0) + jnp.log(1.0 + jnp.exp2(-jnp.abs(z))) * LOG2E
        if masked:
            valid = kcol + j * tk < qpos
            sp = jnp.where(valid, sp, 0.0)
        incl = _dot(sp.astype(BF16), uu_ref[...])
        r = r_ref[...]
        e = z + incl + jnp.concatenate([r] * (tk // LANES), axis=1)
        if masked:
            e = jnp.where(valid, e, NEG)
        acc_ref[...] = acc_ref[...] + _dot(jnp.exp2(e).astype(BF16), vj)
        r_ref[...] = r + incl[:, 0:1]

    per_q = tq // tk
    for u in range(per_q):
        visit((i + 1) * per_q - 1 - u, True)

    n_full = i * per_q
    if per_q == 2:
        odd = i % 2

        @pl.when(odd == 1)
        def _():
            for u in range(2):
                visit(n_full - 1 - u, False)

        top = n_full - 2 * odd

        def body(t, carry):
            for u in range(4):
                visit(top - 1 - 4 * t - u, False)
            return carry

        lax.fori_loop(0, top // 4, body, 0)
    else:
        def body(jj, carry):
            for u in range(per_q):
                visit(n_full - 1 - jj * per_q - u, False)
            return carry

        lax.fori_loop(0, i, body, 0)
    half0 = lax.broadcasted_iota(jnp.int32, (tq, LANES), 1) < HEAD_DIM
    o_ref[...] = jnp.where(half0, acc_ref[0:tq, :], acc_ref[tq:m, :]).astype(BF16)


def _sb_prompt(qkv, bias2, uu, batch, seq_len, n_heads, tq, tk):
    n = qkv.shape[0]
    nq = seq_len // tq
    pairs = n_heads // 2
    kcol0 = n_heads
    vcol0 = n_heads + pairs
    return pl.pallas_call(
        functools.partial(_sb_prompt_kernel, tk=tk),
        out_shape=jax.ShapeDtypeStruct((n, n_heads * HEAD_DIM), BF16),
        grid=(batch, pairs, nq),
        in_specs=[pl.BlockSpec(memory_space=pltpu.SMEM),
                  pl.BlockSpec((tq, 2 * LANES), lambda b, p, i: (b * nq + i, p)),
                  pl.BlockSpec((seq_len, LANES), lambda b, p, i: (b, kcol0 + p)),
                  pl.BlockSpec((seq_len, LANES), lambda b, p, i: (b, vcol0 + p)),
                  pl.BlockSpec(uu.shape, lambda b, p, i: (0, 0))],
        out_specs=pl.BlockSpec((tq, LANES), lambda b, p, i: (b * nq + i, p)),
        scratch_shapes=[pltpu.VMEM((2 * tq, LANES), BF16), pltpu.VMEM((2 * tq, tk), F32),
                        pltpu.VMEM((2 * tq, LANES), F32), pltpu.VMEM((2 * tq, LANES), F32)],
        compiler_params=_params("parallel", "parallel", "arbitrary"),
        name="sb_prompt",
    )(bias2, qkv, qkv, qkv, uu)


def _sb_weights(z, uu, r_prev, valid):
    kb = z.shape[1]
    lk = -_softplus(z)
    if valid is not None:
        lk = jnp.where(valid, lk, 0.0)
    both = _dot(_hi_lo(lk, 1), uu)
    e = z + both[:, :kb] + r_prev
    if valid is not None:
        e = jnp.where(valid, e, NEG)
    return jnp.exp(e).astype(BF16), both[:, kb:]


def _sb_sample_kernel(pt_ref, q_ref, kn_ref, vn_ref, bias_ref, mask_ref, uu_ref, *rest,
                      n_heads, n_tok, tok0, pages_per_step):
    k_refs = rest[:pages_per_step]
    v_refs = rest[pages_per_step:2 * pages_per_step]
    o_ref, qb_ref, acc_ref, r_ref = rest[2 * pages_per_step:]
    hw, ps = k_refs[0].shape
    step = pl.program_id(1)
    rows = n_tok * n_heads

    @pl.when(step == 0)
    def _():
        q = q_ref[...]
        for t in range(n_tok):
            qb_ref[t * n_heads:(t + 1) * n_heads, :] = (
                q[tok0 + t:tok0 + t + 1, :] * mask_ref[...]).astype(BF16)
        zeros = jnp.zeros((ps - SLOTS, hw), F32)
        kn = jnp.concatenate([kn_ref[...], zeros], axis=0).astype(BF16)
        vn = jnp.concatenate([vn_ref[...], zeros], axis=0).astype(BF16)
        tq = lax.broadcasted_iota(jnp.int32, (rows, ps), 0) // n_heads
        j = lax.broadcasted_iota(jnp.int32, (rows, ps), 1)
        a, tot = _sb_weights(_dotg(qb_ref[...], kn, NT) + bias_ref[...], uu_ref[...],
                             jnp.zeros((rows, ps), F32), (j >= tok0) & (j < tok0 + tq))
        acc_ref[...] = _dot(a, vn)
        r_ref[...] = tot

    group = 2 if pages_per_step % 2 == 0 else 1
    side_by_side = lambda refs, g: jnp.concatenate(
        [refs[g + u][...].astype(BF16) for u in range(group)], axis=1)
    zs = []
    for g in range(0, pages_per_step, group):
        z = _dot(qb_ref[...], side_by_side(k_refs, g)) + bias_ref[...]
        zs += [z[:, u * ps:(u + 1) * ps] for u in range(group)]
    boths = [_dot(_hi_lo(-_softplus(z), 1), uu_ref[...]) for z in zs]
    r = r_ref[...]
    acc = acc_ref[...]
    for g in range(0, pages_per_step, group):
        a = []
        for u in range(group):
            a.append(jnp.exp(zs[g + u] + boths[g + u][:, :ps] + r).astype(BF16))
            r = r + boths[g + u][:, ps:]
        acc = acc + _dotg(jnp.concatenate(a, axis=1), side_by_side(v_refs, g), NT)
    acc_ref[...] = acc
    r_ref[...] = r

    @pl.when(step == pl.num_programs(1) - 1)
    def _():
        o_ref[0:tok0, :] = jnp.zeros((tok0, hw), o_ref.dtype)
        for t in range(n_tok):
            blk = acc_ref[t * n_heads:(t + 1) * n_heads, :] * mask_ref[...]
            o_ref[tok0 + t:tok0 + t + 1, :] = jnp.sum(blk, axis=0, keepdims=True)


def _sb_sample(qkv, pool_k, pool_v, pool_base, page_table, bias_col, head_mask, uu, n_heads,
               n_tok, tok0, *, pages_per_step=16):
    n = qkv.shape[0]
    bs, n_pages = page_table.shape
    _, hw, ps = pool_k.shape
    g = pages_per_step
    while n_pages % g:
        g //= 2
    steps = n_pages // g
    rows = n_tok * n_heads
    pt = page_table.reshape(-1)

    def page_map(gi):
        return lambda b, s, pt_ref: (
            pool_base + pt_ref[b * n_pages + n_pages - 1 - (s * g + gi)], 0, 0)

    page_specs = [pl.BlockSpec((None, hw, ps), page_map(gi)) for gi in range(g)]
    kern = functools.partial(_sb_sample_kernel, n_heads=n_heads, n_tok=n_tok, tok0=tok0,
                             pages_per_step=g)
    grid_spec = pltpu.PrefetchScalarGridSpec(
        num_scalar_prefetch=1,
        grid=(bs, steps),
        in_specs=[pl.BlockSpec((SLOTS, hw), lambda b, s, pt_ref: (b, 0)),
                  pl.BlockSpec((SLOTS, hw), lambda b, s, pt_ref: (b, 1)),
                  pl.BlockSpec((SLOTS, hw), lambda b, s, pt_ref: (b, 2)),
                  pl.BlockSpec((rows, 1), lambda b, s, pt_ref: (0, 0)),
                  pl.BlockSpec((n_heads, hw), lambda b, s, pt_ref: (0, 0)),
                  pl.BlockSpec(uu.shape, lambda b, s, pt_ref: (0, 0))] + page_specs + page_specs,
        out_specs=pl.BlockSpec((SLOTS, hw), lambda b, s, pt_ref: (b, 0)),
        scratch_shapes=[pltpu.VMEM((rows, hw), BF16), pltpu.VMEM((rows, hw), F32),
                        pltpu.VMEM((rows, LANES), F32)],
    )
    return pl.pallas_call(
        kern,
        out_shape=jax.ShapeDtypeStruct((n, hw), F32),
        grid_spec=grid_spec,
        compiler_params=_params("parallel", "arbitrary"),
        name="sb_sample",
    )(pt, qkv, qkv, qkv, bias_col, head_mask, uu, *([pool_k] * g), *([pool_v] * g))


def _expand_heads(v, ex2):
    return _dot(_hi_lo(v, 1), ex2)


def _ssd_diag(acs, cb, xs, same_seq, half_masks, h0):
    acs_t = acs.T
    y = None
    for c in range(2):
        h = h0 + c
        seg = acs[:, h:h + 1] - acs_t[h:h + 1, :]
        m = (cb * jnp.exp(jnp.where(same_seq, seg, NEG))).astype(BF16)
        term = _dot(m, jnp.where(half_masks[c], xs, 0.0).astype(BF16))
        y = term if y is None else y + term
    return y


def _ssd_prompt_kernel(z_ref, x_ref, bc_ref, xh_ref, bch_ref, dt_ref, cwx_ref, cbx_ref, cwb_ref,
                       cbb_ref, dtb_ref, a_ref, d_ref, nw_ref, tt_ref, ex_ref, yg_ref, st_ref,
                       state_ref, *, groups, d_state, lane_chunk):
    lc, di = x_ref.shape
    c_idx = pl.program_id(1)
    not_first = c_idx > 0

    @pl.when(c_idx == 0)
    def _():
        state_ref[...] = jnp.zeros_like(state_ref)

    row = lax.broadcasted_iota(jnp.int32, (SUBLANES, lane_chunk), 0)

    def conv_silu(cur_ref, halo_ref, w_ref, b_ref):
        outs = []
        for c0 in range(0, cur_ref.shape[1], lane_chunk):
            sl = slice(c0, c0 + lane_chunk)
            halo = jnp.where(not_first, halo_ref[:, sl].astype(F32), 0.0)
            outs.append(_silu(_conv_taps(cur_ref[:, sl].astype(F32), halo, w_ref, b_ref, sl, row)))
        return jnp.concatenate(outs, axis=1)

    xc = conv_silu(x_ref, xh_ref, cwx_ref, cbx_ref)
    bcc = conv_silu(bc_ref, bch_ref, cwb_ref, cbb_ref)
    gn = groups * d_state
    dt = _softplus(dt_ref[...] + dtb_ref[...])
    a = dt * a_ref[...]
    acs = _dot(tt_ref[...], _hi_lo(a, 0))
    dt_full = _expand_heads(dt, ex_ref[...])
    acs_full = _expand_heads(acs, ex_ref[...])
    eacs_full = jnp.exp(acs_full)
    te_full = jnp.exp(_expand_heads(acs[lc - 1:lc, :] - acs, ex_ref[...]))
    cd_full = eacs_full[lc - 1:lc, :]
    xdt = xc * dt_full
    ti = lax.broadcasted_iota(jnp.int32, (lc, lc), 0)
    si = lax.broadcasted_iota(jnp.int32, (lc, lc), 1)
    causal = si <= ti
    lane = lax.broadcasted_iota(jnp.int32, (lc, LANES), 1)
    half_masks = (lane < HEAD_DIM, lane >= HEAD_DIM)
    pairs_per_group = di // LANES // groups
    ys = []
    for g in range(groups):
        bg = bcc[:, g * d_state:(g + 1) * d_state]
        cg = bcc[:, gn + g * d_state:gn + (g + 1) * d_state].astype(BF16)
        cb = _dotg(cg, bg.astype(BF16), NT)
        bg_t = bg.T.astype(BF16)
        for pp in range(pairs_per_group):
            p = g * pairs_per_group + pp
            sl = slice(p * LANES, (p + 1) * LANES)
            xs = xdt[:, sl]
            st = state_ref[p]
            y = _ssd_diag(acs, cb, xs, causal, half_masks, 2 * p)
            y = y + _dot(cg, st.astype(BF16)) * eacs_full[:, sl] + xc[:, sl] * d_ref[:, sl]
            state_ref[p] = st * cd_full[:, sl] + _dot(bg_t, (xs * te_full[:, sl]).astype(BF16))
            ys.append(y)
    y = jnp.concatenate(ys, axis=1)
    yz = y * _silu(z_ref[...].astype(F32))
    yg_ref[...] = _rms(yz, nw_ref[...]).astype(BF16)

    @pl.when(c_idx == pl.num_programs(1) - 1)
    def _():
        st_ref[0] = state_ref[...]


def _ssd_prompt(zx, dt, cwx, cbx, cwb, cbb, dtb, a, dfull, nw, tt, ex2, batch, seq_len, chunk,
                groups, d_state):
    n = zx.shape[0]
    di = dfull.shape[1]
    nc = seq_len // chunk
    hb = chunk // SUBLANES
    pairs = di // LANES
    kern = functools.partial(_ssd_prompt_kernel, groups=groups, d_state=d_state, lane_chunk=512)
    cur = lambda b, c: b * nc + c
    halo = lambda b, c: jnp.maximum((b * nc + c) * hb - 1, 0)
    full = lambda arr: pl.BlockSpec(arr.shape, lambda b, c: (0, 0))
    return pl.pallas_call(
        kern,
        out_shape=(jax.ShapeDtypeStruct((n, di), BF16),
                   jax.ShapeDtypeStruct((batch, pairs, d_state, LANES), F32)),
        grid=(batch, nc),
        in_specs=[pl.BlockSpec((chunk, di), lambda b, c: (cur(b, c), 0)),
                  pl.BlockSpec((chunk, di), lambda b, c: (cur(b, c), 1)),
                  pl.BlockSpec((chunk, di), lambda b, c: (cur(b, c), 2)),
                  pl.BlockSpec((SUBLANES, di), lambda b, c: (halo(b, c), 1)),
                  pl.BlockSpec((SUBLANES, di), lambda b, c: (halo(b, c), 2)),
                  pl.BlockSpec((chunk, LANES), lambda b, c: (cur(b, c), 0)),
                  full(cwx), full(cbx), full(cwb), full(cbb), full(dtb), full(a), full(dfull),
                  full(nw), full(tt), full(ex2)],
        out_specs=(pl.BlockSpec((chunk, di), lambda b, c: (cur(b, c), 0)),
                   pl.BlockSpec((1, pairs, d_state, LANES), lambda b, c: (b, 0, 0, 0))),
        scratch_shapes=[pltpu.VMEM((pairs, d_state, LANES), F32)],
        compiler_params=_params("parallel", "arbitrary"),
        name="ssd_prompt",
    )(zx, zx, zx, zx, zx, dt, cwx, cbx, cwb, cbb, dtb, a, dfull, nw, tt, ex2)


def _ssd_sample_kernel(z_ref, x_ref, bc_ref, hx_ref, hbc_ref, dt_ref, cwx_ref, cbx_ref, cwb_ref,
                       cbb_ref, dtb_ref, a_ref, d_ref, nw_ref, ex_ref, sel_ref, st_in_ref,
                       yg_ref, st_out_ref, c_s, b_s, xw_s, ea_s, y_s,
                       *, groups, d_state, tok0, lane_chunk):
    rows, di = x_ref.shape
    bi = pl.program_id(1)
    gn = groups * d_state
    pairs_per_group = di // LANES // groups

    @pl.when(bi == 0)
    def _():
        slot_c = lax.broadcasted_iota(jnp.int32, (rows, lane_chunk), 0) % SLOTS

        def conv_silu(cur_ref, hist_ref, w_ref, b_ref):
            outs = []
            for c0 in range(0, cur_ref.shape[1], lane_chunk):
                sl = slice(c0, c0 + lane_chunk)
                ext = jnp.where(slot_c < tok0, hist_ref[:, sl], cur_ref[:, sl].astype(F32))
                outs.append(_silu(_conv_taps_slots(ext, w_ref, b_ref, sl)))
            return jnp.concatenate(outs, axis=1)

        xc = conv_silu(x_ref, hx_ref, cwx_ref, cbx_ref)
        bcc = conv_silu(bc_ref, hbc_ref, cwb_ref, cbb_ref)
        slot = lax.broadcasted_iota(jnp.int32, (rows, LANES), 0) % SLOTS
        dt = jnp.where(slot >= tok0, _softplus(dt_ref[...] + dtb_ref[...]), 0.0)
        a = dt * a_ref[...]
        acs, rev = a, jnp.zeros_like(a)
        for d in range(1, SLOTS - tok0):
            acs = acs + jnp.where(slot - d >= tok0, pltpu.roll(a, d, 0), 0.0)
            rev = rev + jnp.where(slot + d < SLOTS, pltpu.roll(a, rows - d, 0), 0.0)
        dt_full = _expand_heads(dt, ex_ref[...])
        acs_full = _expand_heads(acs, ex_ref[...])
        te_full = jnp.exp(_expand_heads(rev, ex_ref[...]))
        xdt = xc * dt_full
        ti = lax.broadcasted_iota(jnp.int32, (rows, rows), 0)
        si = lax.broadcasted_iota(jnp.int32, (rows, rows), 1)
        same_seq = (si <= ti) & (si // SLOTS == ti // SLOTS)
        lane = lax.broadcasted_iota(jnp.int32, (rows, LANES), 1)
        half_masks = (lane < HEAD_DIM, lane >= HEAD_DIM)
        ys = []
        for g in range(groups):
            bg = bcc[:, g * d_state:(g + 1) * d_state].astype(BF16)
            cg = bcc[:, gn + g * d_state:gn + (g + 1) * d_state].astype(BF16)
            cb = _dotg(cg, bg, NT)
            for pp in range(pairs_per_group):
                p = g * pairs_per_group + pp
                sl = slice(p * LANES, (p + 1) * LANES)
                ys.append(_ssd_diag(acs, cb, xdt[:, sl], same_seq, half_masks, 2 * p)
                          + xc[:, sl] * d_ref[:, sl])
        y_s[...] = jnp.concatenate(ys, axis=1)
        b_s[...] = bcc[:, :gn]
        c_s[...] = bcc[:, gn:]
        xw_s[...] = xdt * te_full
        ea_s[...] = jnp.exp(acs_full)

    r0 = pl.multiple_of(bi * SLOTS, SLOTS)
    pad = jnp.zeros((SLOTS, LANES), F32)
    cb_rows = c_s[pl.ds(r0, SLOTS), :]
    bb_rows = b_s[pl.ds(r0, SLOTS), :]
    xw_rows = xw_s[pl.ds(r0, SLOTS), :]
    ea_rows = ea_s[pl.ds(r0, SLOTS), :]
    y_off = []
    for g in range(groups):
        gs = slice(g * d_state, (g + 1) * d_state)
        cg = jnp.concatenate([cb_rows[:, gs], pad], axis=0).astype(BF16)
        bg = jnp.concatenate([bb_rows[:, gs], pad], axis=0).astype(BF16)
        for pp in range(pairs_per_group):
            p = g * pairs_per_group + pp
            sl = slice(p * LANES, (p + 1) * LANES)
            st = st_in_ref[0, p]
            y_off.append(_dotg(cg, st.astype(BF16), NT)[:SLOTS] * ea_rows[:, sl])
            decay = _dotg(_hi_lo(ea_rows[:, sl], 0), sel_ref[...], TN)
            xw = jnp.concatenate([xw_rows[:, sl], pad], axis=0).astype(BF16)
            st_out_ref[0, p] = st * decay + _dotg(xw, bg, TN)
    y_s[pl.ds(r0, SLOTS), :] = y_s[pl.ds(r0, SLOTS), :] + jnp.concatenate(y_off, axis=1)

    @pl.when(bi == pl.num_programs(1) - 1)
    def _():
        yz = y_s[...] * _silu(z_ref[...].astype(F32))
        yg_ref[...] = _rms(yz, nw_ref[...]).astype(BF16)


def _ssd_sample(zx, hist, dt, cwx, cbx, cwb, cbb, dtb, a, dfull, nw, ex2, sel, state, groups,
                d_state, tok0, *, bb=16):
    n = zx.shape[0]
    bs = state.shape[0]
    di = dfull.shape[1]
    pairs = di // LANES
    bb = min(bb, bs)
    rows = bb * SLOTS
    gn = groups * d_state
    kern = functools.partial(_ssd_sample_kernel, groups=groups, d_state=d_state, tok0=tok0,
                             lane_chunk=512)
    full = lambda arr: pl.BlockSpec(arr.shape, lambda i, j: (0, 0))
    st_spec = pl.BlockSpec((1, pairs, LANES, d_state), lambda i, j: (i * bb + j, 0, 0, 0))
    return pl.pallas_call(
        kern,
        out_shape=(jax.ShapeDtypeStruct((n, di), BF16),
                   jax.ShapeDtypeStruct(state.shape, F32)),
        grid=(bs // bb, bb),
        in_specs=[pl.BlockSpec((rows, di), lambda i, j: (i, 0)),
                  pl.BlockSpec((rows, di), lambda i, j: (i, 1)),
                  pl.BlockSpec((rows, di), lambda i, j: (i, 2)),
                  pl.BlockSpec((rows, di), lambda i, j: (i, 0)),
                  pl.BlockSpec((rows, di), lambda i, j: (i, 1)),
                  pl.BlockSpec((rows, LANES), lambda i, j: (i, 0)),
                  full(cwx), full(cbx), full(cwb), full(cbb), full(dtb), full(a), full(dfull),
                  full(nw), full(ex2), full(sel), st_spec],
        out_specs=(pl.BlockSpec((rows, di), lambda i, j: (i, 0)), st_spec),
        scratch_shapes=[pltpu.VMEM((rows, gn), F32), pltpu.VMEM((rows, gn), F32),
                        pltpu.VMEM((rows, di), F32), pltpu.VMEM((rows, di), F32),
                        pltpu.VMEM((rows, di), F32)],
        compiler_params=_params("parallel", "arbitrary"),
        name="ssd_sample",
    )(zx, zx, zx, hist, hist, dt, cwx, cbx, cwb, cbb, dtb, a, dfull, nw, ex2, sel, state)


def _widen_heads(w, halves):
    d = w.shape[0]
    h = len(halves)
    onehot = jax.nn.one_hot(jnp.asarray(halves), 2, dtype=w.dtype)
    wide = w.reshape(d, h, 1, HEAD_DIM) * onehot[None, :, :, None]
    return wide.reshape(d, h * LANES)


def _suffix_sum_matrix(n, sign):
    j = jnp.arange(n)[:, None]
    s = jnp.arange(n)[None, :]
    u = sign * jnp.concatenate([(j >= s).astype(F32), jnp.ones((n, LANES), F32)], axis=1)
    return jnp.concatenate([u, u], axis=0).astype(BF16)


def _pad_slots(a, lead):
    k = a.shape[-2]
    pad = [(0, 0)] * (a.ndim - 2) + [(lead, SLOTS - lead - k), (0, 0)]
    a = jnp.pad(a, pad)
    return a.reshape(a.shape[:-3] + (a.shape[-3] * SLOTS, a.shape[-1]))


def kernel(x_prompt, x_sample, mem_prompt, cache_swa_k, cache_swa_v, state_ssm, state_ssm_conv, cache_sb_k, cache_sb_v, cache_mem_k, cache_mem_v, state_ffn_conv, page_table, norm_mix_w, norm_xattn_w, norm_ffn_w, final_norm_w, mem_token_norm_w, swa_wqkv, swa_sinks, swa_wo, ssm_w_in, ssm_conv_w, ssm_conv_b, ssm_dt_bias, ssm_a_log, ssm_d, ssm_norm_w, ssm_w_out, sb_wqkv, sb_logit_bias, sb_wo, mem_wq, mem_wkv, mem_wo, ffn_w_in, ffn_conv_w, ffn_conv_b, ffn_w_out):
    bp, seq, d = x_prompt.shape
    bs, n_tok, _ = x_sample.shape
    depth = norm_mix_w.shape[0]
    tok0 = SLOTS - n_tok
    scale = 1.0 / math.sqrt(HEAD_DIM)
    row = lambda v: v.reshape(1, -1).astype(F32)

    window, swa_kvh = cache_swa_k.shape[2], cache_swa_k.shape[3]
    swa_heads = swa_sinks.shape[1]
    swa_group = swa_heads // swa_kvh
    swa_kvw = swa_kvh * HEAD_DIM
    ssm_heads, ssm_p, d_state = state_ssm.shape[2], state_ssm.shape[3], state_ssm.shape[4]
    d_inner = ssm_heads * ssm_p
    conv_dim = ssm_conv_w.shape[2]
    gn = (conv_dim - d_inner) // 2
    groups = gn // d_state
    ssm_k = ssm_conv_w.shape[1]
    sb_heads = sb_logit_bias.shape[1]
    sb_w = sb_heads * HEAD_DIM
    page_size = cache_sb_k.shape[2]
    n_mem, mem_heads = cache_mem_k.shape[2], cache_mem_k.shape[3]
    mem_w = mem_heads * HEAD_DIM
    d_ff = ffn_conv_w.shape[2]
    ffn_k = ffn_conv_w.shape[1]
    assert ssm_p == HEAD_DIM and d_state == LANES and ssm_heads <= LANES
    assert n_tok + max(ssm_k, ffn_k) - 1 <= SLOTS and page_size == LANES and 2 * gn == d_inner
    chunk = math.gcd(seq, 128)

    sb_tq = next(t for t in (512, 256, 128) if seq % t == 0)
    sb_tk = min(256, sb_tq)
    kk = jnp.arange(sb_tk)
    uu_blk = -(kk[:, None] >= kk[None, :]).astype(BF16)
    uu_page = _suffix_sum_matrix(page_size, 1.0)
    sb_head_mask = (jnp.arange(sb_w)[None, :] // HEAD_DIM == jnp.arange(sb_heads)[:, None]).astype(F32)
    tri = (jnp.arange(chunk)[None, :] <= jnp.arange(chunk)[:, None]).astype(BF16)
    tt = jnp.concatenate([tri, tri], axis=1)
    ex = (jnp.arange(d_inner)[None, :] // ssm_p == jnp.arange(LANES)[:, None]).astype(BF16)
    ex2 = jnp.concatenate([ex, ex], axis=0)
    sel = jnp.zeros((2 * SLOTS, d_state), F32).at[SLOTS - 1].set(1.0).at[2 * SLOTS - 1].set(1.0)
    sel = sel.astype(BF16)

    xp = x_prompt.reshape(bp * seq, d)
    xs = _pad_slots(x_sample, tok0)

    wkv_all = jnp.transpose(mem_wkv, (1, 0, 2)).reshape(d, depth * 2 * mem_w).astype(BF16)
    mem_kv = _rms_linear(mem_prompt.reshape(bp * n_mem, d), row(mem_token_norm_w), wkv_all, F32,
                         name="mem_kv")
    mem_kv5 = mem_kv.reshape(bp, n_mem, depth, 2, mem_heads, HEAD_DIM)
    mem_k_prompt = jnp.moveaxis(mem_kv5[:, :, :, 0], 2, 0)
    mem_v_prompt = jnp.moveaxis(mem_kv5[:, :, :, 1], 2, 0)
    mem_kv_b = mem_kv.astype(BF16)

    ffn_hist = _pad_slots(state_ffn_conv, tok0 - (ffn_k - 1))
    cache_t = lambda c: jnp.transpose(c, (0, 1, 3, 4, 2)).reshape(depth * bs, mem_w, n_mem)
    mem_cache_kt, mem_cache_vt = cache_t(cache_mem_k), cache_t(cache_mem_v)

    outs = {k: [] for k in ("swa_kp", "swa_vp", "swa_ks", "swa_vs", "ssm_sp", "ssm_cp", "ssm_ss",
                            "ssm_cs", "sb_kp", "sb_vp", "sb_ks", "sb_vs", "ffn_cp", "ffn_cs")}

    for i in range(depth):
        kind, j = i % 3, i // 3
        nw_mix = row(norm_mix_w[i])
        if kind == 0:
            halves = [(h // swa_group) % 2 for h in range(swa_heads)]
            qw = swa_heads * HEAD_DIM
            w_all = jnp.concatenate([_widen_heads(swa_wqkv[j][:, :qw] * scale, halves),
                                     swa_wqkv[j][:, qw:]], axis=1).astype(BF16)
            order = [(2 * p + c) * swa_group + r for p in range(swa_kvh // 2)
                     for r in range(swa_group) for c in range(2)]
            wo = swa_wo[j].reshape(swa_heads, HEAD_DIM, d)[jnp.asarray(order)].reshape(qw, d).astype(BF16)
            slopes = jnp.exp2(-8.0 * jnp.arange(1, swa_heads + 1, dtype=F32) / swa_heads)
            sm = jnp.stack([slopes, swa_sinks[j].astype(F32)])
            kcol = swa_heads * LANES

            qkv_p = _rms_linear(xp, nw_mix, w_all, BF16, name="swa_qkv_prompt")
            o_p = _swa_prompt(qkv_p, sm, bp, seq, window, swa_heads, swa_group)
            xp = _linear_res(o_p, wo, xp, name="swa_out_prompt")
            kv_p = qkv_p.reshape(bp, seq, -1)[:, seq - window:, kcol:].astype(F32)
            outs["swa_kp"].append(kv_p[..., :swa_kvw].reshape(bp, window, swa_kvh, HEAD_DIM))
            outs["swa_vp"].append(kv_p[..., swa_kvw:].reshape(bp, window, swa_kvh, HEAD_DIM))

            qkv_s = _rms_linear(xs, nw_mix, w_all, BF16, name="swa_qkv_sample")
            o_s = _swa_sample(qkv_s, sm, cache_swa_k[j].reshape(bs, window, swa_kvw),
                              cache_swa_v[j].reshape(bs, window, swa_kvw), swa_heads, swa_group, tok0)
            xs = _linear_res(o_s, wo, xs, name="swa_out_sample")
            kv_s = qkv_s.reshape(bs, SLOTS, -1)[:, tok0:, kcol:].astype(F32)
            k_new = kv_s[..., :swa_kvw].reshape(bs, n_tok, swa_kvh, HEAD_DIM)
            v_new = kv_s[..., swa_kvw:].reshape(bs, n_tok, swa_kvh, HEAD_DIM)
            outs["swa_ks"].append(jnp.concatenate([cache_swa_k[j][:, n_tok:], k_new], axis=1))
            outs["swa_vs"].append(jnp.concatenate([cache_swa_v[j][:, n_tok:], v_new], axis=1))
        elif kind == 1:
            zxw = 2 * d_inner + 2 * gn
            w_main = ssm_w_in[j][:, :zxw].astype(BF16)
            w_dt = jnp.pad(ssm_w_in[j][:, zxw:], ((0, 0), (0, LANES - ssm_heads))).astype(BF16)
            pad_h = lambda v: jnp.pad(v.astype(F32), (0, LANES - ssm_heads)).reshape(1, LANES)
            cwx, cwb = ssm_conv_w[j][:, :d_inner], ssm_conv_w[j][:, d_inner:]
            cbx, cbb = row(ssm_conv_b[j][:d_inner]), row(ssm_conv_b[j][d_inner:])
            dtb = pad_h(ssm_dt_bias[j])
            a_neg = pad_h(-jnp.exp(ssm_a_log[j].astype(F32)))
            dfull = row(jnp.repeat(ssm_d[j].astype(F32), ssm_p))
            nw_ssm = row(ssm_norm_w[j])
            w_out = ssm_w_out[j].astype(BF16)

            zx_p, dt_p = _rms_linear(xp, nw_mix, w_main, BF16, w_dt, name="ssm_in_prompt")
            yg_p, st_p = _ssd_prompt(zx_p, dt_p, cwx, cbx, cwb, cbb, dtb, a_neg, dfull, nw_ssm, tt,
                                     ex2, bp, seq, chunk, groups, d_state)
            xp = _linear_res(yg_p, w_out, xp, name="ssm_out_prompt")
            outs["ssm_sp"].append(jnp.swapaxes(st_p, 2, 3).reshape(bp, ssm_heads, ssm_p, d_state))
            outs["ssm_cp"].append(
                zx_p.reshape(bp, seq, -1)[:, seq - (ssm_k - 1):, d_inner:].astype(F32))

            zx_s, dt_s = _rms_linear(xs, nw_mix, w_main, BF16, w_dt, name="ssm_in_sample")
            hist = _pad_slots(state_ssm_conv[j], tok0 - (ssm_k - 1))
            yg_s, st_s = _ssd_sample(zx_s, hist, dt_s, cwx, cbx, cwb, cbb, dtb, a_neg, dfull, nw_ssm,
                                     ex2, sel, state_ssm[j].reshape(bs, d_inner // LANES, LANES, d_state),
                                     groups, d_state, tok0)
            xs = _linear_res(yg_s, w_out, xs, name="ssm_out_sample")
            outs["ssm_ss"].append(st_s.reshape(bs, ssm_heads, ssm_p, d_state))
            ext = jnp.concatenate(
                [state_ssm_conv[j], zx_s.reshape(bs, SLOTS, -1)[:, tok0:, d_inner:].astype(F32)], axis=1)
            outs["ssm_cs"].append(ext[:, -(ssm_k - 1):])
        else:
            wq, wk, wv = (sb_wqkv[j][:, t * sb_w:(t + 1) * sb_w] for t in range(3))
            w_p = jnp.concatenate([_widen_heads(wq * (scale * LOG2E), [h % 2 for h in range(sb_heads)]),
                                   wk, wv], axis=1).astype(BF16)
            w_s = jnp.concatenate([wq * scale, wk, wv], axis=1).astype(BF16)
            bias = sb_logit_bias[j].astype(F32)
            wo = sb_wo[j].astype(BF16)
            kcol = sb_heads * LANES

            wkv_t = jnp.concatenate([wk, wv], axis=1).T.astype(BF16)
            qkv_p, k_t, v_t = _rms_linear_t(xp, nw_mix, w_p, wkv_t, 2, seq, name="sb_qkv_prompt")
            o_p = _sb_prompt(qkv_p, bias * LOG2E, uu_blk, bp, seq, sb_heads, sb_tq, sb_tk)
            xp = _linear_res(o_p, wo, xp, name="sb_out_prompt")
            heads_last = lambda t: jnp.transpose(t.reshape(bp, sb_heads, HEAD_DIM, seq), (0, 3, 1, 2))
            outs["sb_kp"].append(heads_last(k_t))
            outs["sb_vp"].append(heads_last(v_t))

            qkv_s = _rms_linear(xs, nw_mix, w_s, F32, name="sb_qkv_sample")
            n_c, n_pool = cache_sb_k.shape[:2]
            pages_t = lambda c: jnp.transpose(c, (0, 1, 3, 4, 2)).reshape(n_c * n_pool, sb_w, page_size)
            o_s = _sb_sample(qkv_s, pages_t(cache_sb_k), pages_t(cache_sb_v), j * n_pool, page_table,
                             jnp.tile(bias, n_tok).reshape(-1, 1), sb_head_mask, uu_page,
                             sb_heads, n_tok, tok0)
            xs = _linear_res(o_s, wo, xs, name="sb_out_sample")
            kv_s = qkv_s.reshape(bs, SLOTS, -1)[:, tok0:, sb_w:]
            outs["sb_ks"].append(kv_s[..., :sb_w].reshape(bs, n_tok, sb_heads, HEAD_DIM))
            outs["sb_vs"].append(kv_s[..., sb_w:].reshape(bs, n_tok, sb_heads, HEAD_DIM))

        nw_x = row(norm_xattn_w[i])
        wq_m = _widen_heads(mem_wq[i] * scale, [h % 2 for h in range(mem_heads)]).astype(BF16)
        wo_m = mem_wo[i].astype(BF16)
        k_m = mem_kv_b[:, (2 * i) * mem_w:(2 * i + 1) * mem_w]
        v_m = mem_kv_b[:, (2 * i + 1) * mem_w:(2 * i + 2) * mem_w]
        xp = _mem_attn_prompt(xp, nw_x, wq_m, k_m, v_m, wo_m, seq, n_mem)
        xs = _mem_attn_sample(xs, nw_x, wq_m, mem_cache_kt, mem_cache_vt, wo_m, i)

        nw_f = row(norm_ffn_w[i])
        w_in = ffn_w_in[i].astype(BF16)
        w_out = ffn_w_out[i].astype(BF16)
        cw, cb = ffn_conv_w[i].astype(F32), row(ffn_conv_b[i])
        gu_p = _rms_linear(xp, nw_f, w_in, BF16, name="ffn_in_prompt")
        xp = _glu_prompt(gu_p, cw, cb, w_out, xp, seq,
                         row(final_norm_w) if i == depth - 1 else None)
        outs["ffn_cp"].append(gu_p.reshape(bp, seq, -1)[:, seq - (ffn_k - 1):, :d_ff].astype(F32))
        gu_s = _rms_linear(xs, nw_f, w_in, BF16, name="ffn_in_sample")
        xs = _glu_sample(gu_s, ffn_hist[i], cw, cb, w_out, xs, tok0)
        ext = jnp.concatenate(
            [state_ffn_conv[i], gu_s.reshape(bs, SLOTS, -1)[:, tok0:, :d_ff].astype(F32)], axis=1)
        outs["ffn_cs"].append(ext[:, -(ffn_k - 1):])

    y_prompt = xp.reshape(bp, seq, d)
    y_sample = _rmsnorm(xs, row(final_norm_w)).reshape(bs, SLOTS, d)[:, tok0:]
    st = lambda k: jnp.stack(outs[k])
    return (y_prompt, y_sample, st("swa_kp"), st("swa_vp"), st("swa_ks"), st("swa_vs"),
            st("ssm_sp"), st("ssm_cp"), st("ssm_ss"), st("ssm_cs"),
            st("sb_kp"), st("sb_vp"), st("sb_ks"), st("sb_vs"),
            mem_k_prompt, mem_v_prompt, st("ffn_cp"), st("ffn_cs"))
```
